```python
import jax
import jax.numpy as jnp
from jax import lax
import numpy as np

D_MODEL = 2048
BATCH = 16
SEQ = 2048
DEPTH = 4

GRID_W = 64
CTX_LEN = 256
CONV_W = 1024
CONV_K = 31
RET_HEADS = 8
RET_HEAD_DIM = 128
RET_W = RET_HEADS * RET_HEAD_DIM
MIX_W = CONV_W + RET_W
IN_W = 2 * CONV_W + 4 * RET_W
RET_CHUNK = 128
FFN_DENSE = 5632
N_EXPERTS = 8
TOP_K = 2
FFN_EXPERT = 4096
N_MOE = DEPTH // 2
N_DENSE = DEPTH - N_MOE
ROPE_BASE = 10000.0
NORM_EPS = 1e-6

kernel_name = 'hybrid_conv_retention_moe_dit'


def rms_norm(x, w):
    xf = x.astype(jnp.float32)
    y = xf * lax.rsqrt(jnp.mean(xf * xf, axis=-1, keepdims=True) + NORM_EPS)
    return (y * w.astype(jnp.float32)).astype(x.dtype)


def layer_norm_f32(x, w, b):
    xf = x.astype(jnp.float32)
    mu = jnp.mean(xf, axis=-1, keepdims=True)
    var = jnp.mean(jnp.square(xf - mu), axis=-1, keepdims=True)
    return (xf - mu) * lax.rsqrt(var + NORM_EPS) * w.astype(jnp.float32) + b.astype(jnp.float32)


def grid_rope(n):
    rows = n // GRID_W
    row = jnp.repeat(jnp.arange(rows, dtype=jnp.float32), GRID_W)
    col = jnp.tile(jnp.arange(GRID_W, dtype=jnp.float32), rows)
    n_freq = RET_HEAD_DIM // 4
    inv_freq = ROPE_BASE ** (-jnp.arange(n_freq, dtype=jnp.float32) / n_freq)
    ang = jnp.stack([row[:, None] * inv_freq, col[:, None] * inv_freq], axis=1)
    ang = jnp.broadcast_to(ang[:, :, None, :], (rows * GRID_W, 2, 2, n_freq))
    ang = ang.reshape(rows * GRID_W, RET_HEAD_DIM)
    return jnp.cos(ang), jnp.sin(ang)


def apply_rope(t, cos, sin):
    tr = t.reshape(*t.shape[:-1], 2, 2, RET_HEAD_DIM // 4)
    rot = jnp.stack([-tr[..., 1, :], tr[..., 0, :]], axis=-2).reshape(t.shape)
    return t * cos[:, None, :].astype(t.dtype) + rot * sin[:, None, :].astype(t.dtype)


def to_heads(t):
    return t.reshape(t.shape[0], t.shape[1], RET_HEADS, RET_HEAD_DIM)


def flip_seq(t):
    return jnp.flip(t, axis=1)


def chunk_retention(q, k, v, log_g, s0, strict):
    bsz, n, h, dh = q.shape
    nc = n // RET_CHUNK

    def to_chunks(t):
        t = t.astype(jnp.float32).reshape(bsz, nc, RET_CHUNK, h, dh)
        return t.transpose(1, 0, 3, 2, 4)

    idx = jnp.arange(RET_CHUNK, dtype=jnp.float32)
    diff = idx[:, None] - idx[None, :]
    mask = (diff > 0) if strict else (diff >= 0)
    intra = jnp.where(mask[None], jnp.exp(jnp.maximum(diff, 0.0)[None] * log_g[:, None, None]), 0.0)
    q_dec = jnp.exp((idx + 1.0)[None, :] * log_g[:, None])[..., None]
    k_dec = jnp.exp((RET_CHUNK - 1.0 - idx)[None, :] * log_g[:, None])[..., None]
    c_dec = jnp.exp(RET_CHUNK * log_g)[:, None, None]

    def step(state, qkv):
        qc, kc, vc = qkv
        scores = jnp.einsum('bhid,bhjd->bhij', qc, kc) * intra
        o = (jnp.einsum('bhij,bhjv->bhiv', scores, vc)
             + jnp.einsum('bhid,bhdv->bhiv', qc, state) * q_dec)
        state = state * c_dec + jnp.einsum('bhjd,bhjv->bhdv', kc * k_dec, vc)
        return state, o

    state, o = lax.scan(step, s0, (to_chunks(q), to_chunks(k), to_chunks(v)))
    o = o.transpose(1, 0, 3, 2, 4).reshape(bsz, n, h, dh)
    return o, state


def retention_end_state(k, v, log_g, reverse):
    n = k.shape[1]
    pos = jnp.arange(n, dtype=jnp.float32)
    expo = pos if reverse else (n - 1.0 - pos)
    w = jnp.exp(log_g[:, None] * expo[None, :])
    return jnp.einsum('bnhd,bnhv,hn->bhdv', k.astype(jnp.float32), v.astype(jnp.float32), w)


def retention_readout(o, gate, gn_w, gn_b):
    mu = jnp.mean(o, axis=-1, keepdims=True)
    var = jnp.mean(jnp.square(o - mu), axis=-1, keepdims=True)
    on = ((o - mu) * lax.rsqrt(var + NORM_EPS)).reshape(o.shape[0], o.shape[1], RET_W)
    on = on * gn_w.astype(jnp.float32) + gn_b.astype(jnp.float32)
    return on.astype(gate.dtype) * jax.nn.silu(gate)


def conformer_conv(a, b, conv_w, conv_b, ln_w, ln_b):
    u = a * jax.nn.sigmoid(b)
    u = lax.conv_general_dilated(
        u, conv_w[:, None, :].astype(u.dtype), window_strides=(1,),
        padding=[(CONV_K // 2, CONV_K // 2)], dimension_numbers=('NWC', 'WIO', 'NWC'),
        feature_group_count=CONV_W) + conv_b.astype(u.dtype)
    return jax.nn.silu(layer_norm_f32(u, ln_w, ln_b)).astype(a.dtype)


def hybrid_mixer(h_lat, h_ctx, cos, sin, w_in, conv_w, conv_b, conv_ln_w, conv_ln_b,
                 decay_logit, gn_w, gn_b, w_out, need_ctx):
    bsz = h_lat.shape[0]
    log_g = jax.nn.log_sigmoid(decay_logit.astype(jnp.float32))
    q_off = 2 * CONV_W
    k_off = q_off + RET_W
    v_off = k_off + RET_W
    g_off = v_off + RET_W
    cuts = [CONV_W, q_off, k_off, v_off, g_off]
    k_scale = RET_HEAD_DIM ** -0.5

    la, lb, lq, lk, lv, lgate = jnp.split(h_lat @ w_in, cuts, axis=-1)
    lq = apply_rope(to_heads(lq), cos, sin)
    lk = apply_rope(to_heads(lk), cos, sin) * k_scale
    lv = to_heads(lv)

    if need_ctx:
        ca, cb, cq, ck, cv, cgate = jnp.split(h_ctx @ w_in, cuts, axis=-1)
        cq, ck, cv = to_heads(cq), to_heads(ck) * k_scale, to_heads(cv)
        zero = jnp.zeros((bsz, RET_HEADS, RET_HEAD_DIM, RET_HEAD_DIM), jnp.float32)
        co_f, s_f = chunk_retention(cq, ck, cv, log_g[0], zero, False)
        co_b, s_b = chunk_retention(flip_seq(cq), flip_seq(ck), flip_seq(cv), log_g[1], zero, True)
    else:
        ck, cv = jnp.split(h_ctx @ w_in[:, k_off:g_off], 2, axis=-1)
        ck, cv = to_heads(ck) * k_scale, to_heads(cv)
        s_f = retention_end_state(ck, cv, log_g[0], False)
        s_b = retention_end_state(ck, cv, log_g[1], True)

    o_f, _ = chunk_retention(lq, lk, lv, log_g[0], s_f, False)
    o_b, _ = chunk_retention(flip_seq(lq), flip_seq(lk), flip_seq(lv), log_g[1], s_b, True)
    y_lat = jnp.concatenate(
        [conformer_conv(la, lb, conv_w, conv_b, conv_ln_w, conv_ln_b),
         retention_readout(o_f + flip_seq(o_b), lgate, gn_w, gn_b)], axis=-1) @ w_out

    y_ctx = None
    if need_ctx:
        y_ctx = jnp.concatenate(
            [conformer_conv(ca, cb, conv_w, conv_b, conv_ln_w, conv_ln_b),
             retention_readout(co_f + flip_seq(co_b), cgate, gn_w, gn_b)], axis=-1) @ w_out
    return y_lat, y_ctx


def swiglu(h, w1, w3, w2):
    return (jax.nn.silu(h @ w1) * (h @ w3)) @ w2


def moe_swiglu(h, w_router, b_router, w1, w3, w2):
    logits = (h @ w_router).astype(jnp.float32) + b_router.astype(jnp.float32)
    top_logit, top_idx = lax.top_k(logits, TOP_K)
    top_w = jax.nn.softmax(top_logit, axis=-1)
    combine = jnp.einsum('bnk,bnke->bne', top_w,
                         jax.nn.one_hot(top_idx, N_EXPERTS, dtype=jnp.float32)).astype(h.dtype)
    y = jnp.zeros_like(h)
    for e in range(N_EXPERTS):
        y = y + combine[..., e:e + 1] * swiglu(h, w1[e], w3[e], w2[e])
    return y


def channel_mixer(h, i, ffn_w1, ffn_w3, ffn_w2, moe_router, moe_router_b, moe_w1, moe_w3, moe_w2):
    j = i // 2
    if i % 2 == 0:
        return swiglu(h, ffn_w1[j], ffn_w3[j], ffn_w2[j])
    return moe_swiglu(h, moe_router[j], moe_router_b[j], moe_w1[j], moe_w3[j], moe_w2[j])


def setup_inputs(seed: int = 0) -> dict:
    key = jax.random.key(seed)
    ks = jax.random.split(key, 26)
    D = D_MODEL

    def nrm(k, shape, s):
        return jax.random.normal(k, shape, jnp.float32) * s

    gamma = 1.0 - 2.0 ** (-5.0 - np.arange(RET_HEADS, dtype=np.float32))
    base_logit = jnp.asarray(np.log(gamma / (1.0 - gamma)), jnp.float32)
    return {
        'x': nrm(ks[0], (BATCH, SEQ, D), 1.0),
        'c': nrm(ks[1], (BATCH, D), 1.0),
        'ctx': nrm(ks[2], (BATCH, CTX_LEN, D), 1.0),
        'c_ctx': nrm(ks[3], (D,), 1.0),
        'w_mod': nrm(ks[4], (DEPTH, D, 6 * D), 0.5 * D ** -0.5),
        'b_mod': nrm(ks[5], (DEPTH, 6 * D), 0.02),
        'norm1_w': 1.0 + nrm(ks[6], (DEPTH, D), 0.02),
        'norm2_w': 1.0 + nrm(ks[7], (DEPTH, D), 0.02),
        'w_in': nrm(ks[8], (DEPTH, D, IN_W), D ** -0.5),
        'conv_w': nrm(ks[9], (DEPTH, CONV_K, CONV_W), CONV_K ** -0.5),
        'conv_b': nrm(ks[10], (DEPTH, CONV_W), 0.02),
        'conv_ln_w': 1.0 + nrm(ks[11], (DEPTH, CONV_W), 0.02),
        'conv_ln_b': nrm(ks[12], (DEPTH, CONV_W), 0.02),
        'ret_decay_logit': base_logit[None, None, :] + nrm(ks[13], (DEPTH, 2, RET_HEADS), 0.1),
        'ret_gn_w': 1.0 + nrm(ks[14], (DEPTH, RET_W), 0.02),
        'ret_gn_b': nrm(ks[15], (DEPTH, RET_W), 0.02),
        'w_out': nrm(ks[16], (DEPTH, MIX_W, D), MIX_W ** -0.5),
        'ffn_w1': nrm(ks[17], (N_DENSE, D, FFN_DENSE), D ** -0.5),
        'ffn_w3': nrm(ks[18], (N_DENSE, D, FFN_DENSE), D ** -0.5),
        'ffn_w2': nrm(ks[19], (N_DENSE, FFN_DENSE, D), FFN_DENSE ** -0.5),
        'moe_router': nrm(ks[20], (N_MOE, D, N_EXPERTS), D ** -0.5),
        'moe_router_b': nrm(ks[21], (N_MOE, N_EXPERTS), 0.01),
        'moe_w1': nrm(ks[22], (N_MOE, N_EXPERTS, D, FFN_EXPERT), D ** -0.5),
        'moe_w3': nrm(ks[23], (N_MOE, N_EXPERTS, D, FFN_EXPERT), D ** -0.5),
        'moe_w2': nrm(ks[24], (N_MOE, N_EXPERTS, FFN_EXPERT, D), FFN_EXPERT ** -0.5),
        'final_w': 1.0 + nrm(ks[25], (D,), 0.02),
    }


def reference(x, c, ctx, c_ctx, w_mod, b_mod, norm1_w, norm2_w, w_in, conv_w, conv_b,
              conv_ln_w, conv_ln_b, ret_decay_logit, ret_gn_w, ret_gn_b, w_out,
              ffn_w1, ffn_w3, ffn_w2, moe_router, moe_router_b, moe_w1, moe_w3, moe_w2, final_w):
    cos, sin = grid_rope(x.shape[1])
    for i in range(DEPTH):
        last = i == DEPTH - 1
        m_lat = (jax.nn.silu(c) @ w_mod[i] + b_mod[i])[:, None, :]
        m_ctx = (jax.nn.silu(c_ctx) @ w_mod[i] + b_mod[i])[None, None, :]
        sh1, sc1, g1, sh2, sc2, g2 = jnp.split(m_lat, 6, axis=-1)
        csh1, csc1, cg1, csh2, csc2, cg2 = jnp.split(m_ctx, 6, axis=-1)

        h_lat = rms_norm(x, norm1_w[i]) * (1.0 + sc1) + sh1
        h_ctx = rms_norm(ctx, norm1_w[i]) * (1.0 + csc1) + csh1
        y_lat, y_ctx = hybrid_mixer(h_lat, h_ctx, cos, sin, w_in[i], conv_w[i], conv_b[i],
                                    conv_ln_w[i], conv_ln_b[i], ret_decay_logit[i],
                                    ret_gn_w[i], ret_gn_b[i], w_out[i], not last)
        x = x + g1 * y_lat
        h_lat = rms_norm(x, norm2_w[i]) * (1.0 + sc2) + sh2
        x = x + g2 * channel_mixer(h_lat, i, ffn_w1, ffn_w3, ffn_w2, moe_router, moe_router_b,
                                   moe_w1, moe_w3, moe_w2)
        if not last:
            ctx = ctx + cg1 * y_ctx
            h_ctx = rms_norm(ctx, norm2_w[i]) * (1.0 + csc2) + csh2
            ctx = ctx + cg2 * channel_mixer(h_ctx, i, ffn_w1, ffn_w3, ffn_w2, moe_router,
                                            moe_router_b, moe_w1, moe_w3, moe_w2)
    return rms_norm(x, final_w)
```

```python
import functools

import jax
import jax.numpy as jnp
from jax import lax
from jax.experimental import pallas as pl
from jax.experimental.pallas import tpu as pltpu

F32 = jnp.float32
BF16 = jnp.bfloat16
I32 = jnp.int32

GRID_W = 64
ROPE_BASE = 10000.0
NORM_EPS = 1e-6
TOP_K = 2
LANES = 128
SUBLANES = 8
VMEM_LIMIT_BYTES = 56 * 1024 * 1024


def _cparams(*sem):
    return pltpu.CompilerParams(dimension_semantics=sem, vmem_limit_bytes=VMEM_LIMIT_BYTES)


def _tile(pref, *dims):
    t = pref
    while any(d % t for d in dims):
        t //= 2
    return t


def _sigmoid(v):
    return 1.0 / (1.0 + jnp.exp(-v))


def _log_sigmoid(v):
    return jnp.minimum(v, 0.0) - jnp.log1p(jnp.exp(-jnp.abs(v)))


def _mod_body(c_ref, w_ref, b_ref, o_ref):
    c = c_ref[...]
    s = (c * _sigmoid(c)).astype(BF16)
    o_ref[0] = jnp.dot(s, w_ref[0].astype(BF16), preferred_element_type=F32) + b_ref[0]


def _modulation(c_all, w_mod, b_mod):
    depth, d, n6 = w_mod.shape
    rm = c_all.shape[0]
    tn = _tile(1024, n6)
    return pl.pallas_call(
        _mod_body,
        grid=(depth, n6 // tn),
        in_specs=[
            pl.BlockSpec((rm, d), lambda l, j: (0, 0)),
            pl.BlockSpec((1, d, tn), lambda l, j: (l, 0, j)),
            pl.BlockSpec((1, 1, tn), lambda l, j: (l, 0, j)),
        ],
        out_specs=pl.BlockSpec((1, rm, tn), lambda l, j: (l, 0, j)),
        out_shape=jax.ShapeDtypeStruct((depth, rm, n6), F32),
        compiler_params=_cparams("parallel", "parallel"),
        name="modulation",
    )(c_all, w_mod, b_mod.reshape(depth, 1, n6))


def _mod_row_fn(tm, first_row, n_ctx_rows, seq, nb):
    t0 = first_row // tm
    n_ctx_tiles = n_ctx_rows // tm
    per_seq = seq // tm

    def f(i):
        g = i + t0
        return jnp.where(g < n_ctx_tiles, nb, (g - n_ctx_tiles) // per_seq)

    return f


def _modulated_norm(x, nw, sc, sh):
    ms = jnp.mean(x * x, axis=-1, keepdims=True)
    y = x * lax.rsqrt(ms + NORM_EPS) * nw
    return y * (1.0 + sc) + sh


def _norm_mod_body(x_ref, nw_ref, sc_ref, sh_ref, h_ref):
    h = _modulated_norm(x_ref[...], nw_ref[...], sc_ref[0], sh_ref[0])
    h_ref[...] = h.astype(h_ref.dtype)


def _router_top2(h, wr, br, n_exp):
    logits = jnp.dot(h, wr, preferred_element_type=F32, precision=lax.Precision.HIGHEST) + br
    lane = lax.broadcasted_iota(I32, logits.shape, 1)
    neg = jnp.float32(-jnp.inf)
    logits = jnp.where(lane < n_exp, logits, neg)
    m1 = jnp.max(logits, axis=-1, keepdims=True)
    i1 = jnp.min(jnp.where(logits == m1, lane, LANES), axis=-1, keepdims=True)
    rest = jnp.where(lane == i1, neg, logits)
    m2 = jnp.max(rest, axis=-1, keepdims=True)
    i2 = jnp.min(jnp.where(rest == m2, lane, LANES), axis=-1, keepdims=True)
    e = jnp.exp(m2 - m1)
    w1 = 1.0 / (1.0 + e)
    w2 = e / (1.0 + e)
    idx = jnp.where(lane == 0, i1, jnp.where(lane == 1, i2, 0))
    wgt = jnp.where(lane == 0, w1, jnp.where(lane == 1, w2, 0.0))
    return idx, wgt


def _norm_mod_router_body(n_exp, x_ref, nw_ref, sc_ref, sh_ref, wr_ref, br_ref,
                          h_ref, idx_ref, wgt_ref):
    h = _modulated_norm(x_ref[...], nw_ref[...], sc_ref[0], sh_ref[0])
    h_ref[...] = h.astype(h_ref.dtype)
    idx, wgt = _router_top2(h, wr_ref[...], br_ref[...], n_exp)
    idx_ref[...] = idx
    wgt_ref[...] = wgt


def _norm_mod(x, nw, sc, sh, geom, first_row, out_dtype, router=None):
    n_ctx_rows, seq, nb = geom
    r, d = x.shape
    rows = r - first_row
    tm = _tile(256, rows, first_row if first_row else rows, n_ctx_rows, seq)
    t0 = first_row // tm
    mrow = _mod_row_fn(tm, first_row, n_ctx_rows, seq, nb)
    in_specs = [
        pl.BlockSpec((tm, d), lambda i: (i + t0, 0)),
        pl.BlockSpec((1, d), lambda i: (0, 0)),
        pl.BlockSpec((1, 1, d), lambda i: (mrow(i), 0, 0)),
        pl.BlockSpec((1, 1, d), lambda i: (mrow(i), 0, 0)),
    ]
    args = [x, nw.reshape(1, d), sc, sh]
    row_spec = pl.BlockSpec((tm, d), lambda i: (i, 0))
    if router is None:
        return pl.pallas_call(
            _norm_mod_body, grid=(rows // tm,), in_specs=in_specs, out_specs=row_spec,
            out_shape=jax.ShapeDtypeStruct((rows, d), out_dtype),
            compiler_params=_cparams("parallel"), name="norm_mod",
        )(*args)
    wr, br = router
    n_exp = wr.shape[1]
    wr_p = jnp.zeros((d, LANES), F32).at[:, :n_exp].set(wr)
    br_p = jnp.zeros((1, LANES), F32).at[0, :n_exp].set(br)
    in_specs += [pl.BlockSpec((d, LANES), lambda i: (0, 0)),
                 pl.BlockSpec((1, LANES), lambda i: (0, 0))]
    lane_spec = pl.BlockSpec((tm, LANES), lambda i: (i, 0))
    return pl.pallas_call(
        functools.partial(_norm_mod_router_body, n_exp),
        grid=(rows // tm,), in_specs=in_specs,
        out_specs=[row_spec, lane_spec, lane_spec],
        out_shape=[jax.ShapeDtypeStruct((rows, d), out_dtype),
                   jax.ShapeDtypeStruct((rows, LANES), I32),
                   jax.ShapeDtypeStruct((rows, LANES), F32)],
        compiler_params=_cparams("parallel"), name="norm_mod_router",
    )(*args, wr_p, br_p)


def _matmul_body(a_ref, w_ref, o_ref):
    o_ref[...] = jnp.dot(a_ref[...], w_ref[...], preferred_element_type=F32).astype(o_ref.dtype)


def _matmul(a, w, out_dtype):
    m, k = a.shape
    n = w.shape[1]
    tm = _tile(1024, m)
    tn = _tile(1024, n)
    return pl.pallas_call(
        _matmul_body, grid=(m // tm, n // tn),
        in_specs=[pl.BlockSpec((tm, k), lambda i, j: (i, 0)),
                  pl.BlockSpec((k, tn), lambda i, j: (0, j))],
        out_specs=pl.BlockSpec((tm, tn), lambda i, j: (i, j)),
        out_shape=jax.ShapeDtypeStruct((m, n), out_dtype),
        compiler_params=_cparams("parallel", "parallel"), name="in_proj",
    )(a, w)


CONV_ROWS = 32
CONV_PAD = 16


def _conv_body(n, kw, a_ref, b_ref, cw_ref, cb_ref, lw_ref, lb_ref, o_ref, g_ref):
    cw = a_ref.shape[1]
    half = kw // 2
    zeros = jnp.zeros((CONV_PAD, cw), F32)
    g_ref[pl.ds(0, CONV_PAD), :] = zeros
    g_ref[pl.ds(CONV_PAD + n, CONV_PAD), :] = zeros

    def glu_step(i, carry):
        r = pl.multiple_of(i * CONV_ROWS, CONV_ROWS)
        a = a_ref[pl.ds(r, CONV_ROWS), :].astype(F32)
        b = b_ref[pl.ds(r, CONV_ROWS), :].astype(F32)
        g_ref[pl.ds(r + CONV_PAD, CONV_ROWS), :] = a * _sigmoid(b)
        return carry

    lax.fori_loop(0, n // CONV_ROWS, glu_step, 0)

    def conv_step(i, carry):
        r = pl.multiple_of(i * CONV_ROWS, CONV_ROWS)
        acc = jnp.zeros((CONV_ROWS, cw), F32) + cb_ref[...]
        win = g_ref[pl.ds(r, CONV_ROWS + 2 * CONV_PAD), :]
        for k in range(kw):
            s = CONV_PAD - half + k
            acc = acc + win[s:s + CONV_ROWS, :] * cw_ref[pl.ds(k, 1), :]
        mu = jnp.mean(acc, axis=-1, keepdims=True)
        cen = acc - mu
        var = jnp.mean(cen * cen, axis=-1, keepdims=True)
        y = cen * lax.rsqrt(var + NORM_EPS) * lw_ref[...] + lb_ref[...]
        o_ref[pl.ds(r, CONV_ROWS), :] = (y * _sigmoid(y)).astype(o_ref.dtype)
        return carry

    lax.fori_loop(0, n // CONV_ROWS, conv_step, 0)


def _conformer_conv(u, first_row, n, nb, conv_w, conv_b, ln_w, ln_b):
    kw, cw = conv_w.shape
    assert kw // 2 < CONV_PAD and n % CONV_ROWS == 0 and first_row % n == 0
    b0 = first_row // n
    return pl.pallas_call(
        functools.partial(_conv_body, n, kw),
        grid=(nb,),
        in_specs=[
            pl.BlockSpec((n, cw), lambda b: (b + b0, 0)),
            pl.BlockSpec((n, cw), lambda b: (b + b0, 1)),
            pl.BlockSpec((kw, cw), lambda b: (0, 0)),
            pl.BlockSpec((1, cw), lambda b: (0, 0)),
            pl.BlockSpec((1, cw), lambda b: (0, 0)),
            pl.BlockSpec((1, cw), lambda b: (0, 0)),
        ],
        out_specs=pl.BlockSpec((n, cw), lambda b: (b, 0)),
        out_shape=jax.ShapeDtypeStruct((nb * n, cw), BF16),
        scratch_shapes=[pltpu.VMEM((n + 2 * CONV_PAD, cw), F32)],
        compiler_params=_cparams("parallel"), name="conformer_conv",
    )(u, u, conv_w, conv_b.reshape(1, cw), ln_w.reshape(1, cw), ln_b.reshape(1, cw))


RET_CHUNK_ROWS = 256


def _retention_body(n, chunk, rope, has_init, k_scale, *refs):
    dl_ref, q_ref, k_ref, v_ref, gate_ref, gw_ref, gb_ref = refs[:7]
    pos = 7
    if rope:
        cos_ref, sin_ref = refs[pos:pos + 2]
        pos += 2
    if has_init:
        s0_ref = refs[pos]
        pos += 1
    o_ref, sout_ref, qs_ref, ks_ref, acc_ref = refs[pos:pos + 5]
    dh = q_ref.shape[1]
    nc = n // chunk
    h = pl.program_id(1)

    def rows(c):
        return pl.ds(pl.multiple_of(c * chunk, chunk), chunk)

    def prep(c, carry):
        rs = rows(c)
        q = q_ref[rs, :].astype(F32)
        k = k_ref[rs, :].astype(F32)
        if rope:
            lane = lax.broadcasted_iota(I32, (chunk, dh), 1)
            first = (lane % (dh // 2)) < (dh // 4)
            cos = cos_ref[rs, :]
            sin = sin_ref[rs, :]

            def rot(t):
                return jnp.where(first, -pltpu.roll(t, dh - dh // 4, 1), pltpu.roll(t, dh // 4, 1))

            q = q * cos + rot(q) * sin
            k = k * cos + rot(k) * sin
        qs_ref[rs, :] = q.astype(BF16)
        ks_ref[rs, :] = k * k_scale
        return carry

    lax.fori_loop(0, nc, prep, 0)

    def log_decay(direction, shape):
        return _log_sigmoid(jnp.full(shape, dl_ref[direction, h], F32))

    ii = lax.broadcasted_iota(I32, (chunk, chunk), 0)
    jj = lax.broadcasted_iota(I32, (chunk, chunk), 1)
    diff = (ii - jj).astype(F32)
    decay = jnp.where(diff >= 0.0,
                      jnp.exp(jnp.maximum(diff, 0.0) * log_decay(0, (chunk, chunk))),
                      jnp.exp(jnp.maximum(-diff, 0.0) * log_decay(1, (chunk, chunk))))
    idx = lax.broadcasted_iota(I32, (chunk, dh), 0).astype(F32)
    lg_f = log_decay(0, (chunk, dh))
    lg_b = log_decay(1, (chunk, dh))
    qdec_f = jnp.exp((idx + 1.0) * lg_f)
    kdec_f = jnp.exp((chunk - 1.0 - idx) * lg_f)
    qdec_b = jnp.exp((chunk - idx) * lg_b)
    kdec_b = jnp.exp(idx * lg_b)
    cdec_f = jnp.exp(chunk * log_decay(0, (dh, dh)))
    cdec_b = jnp.exp(chunk * log_decay(1, (dh, dh)))

    def kv_outer(kd, v):
        return lax.dot_general(kd.astype(BF16), v, (((0,), (0,)), ((), ())),
                               preferred_element_type=F32)

    def fwd(c, state):
        rs = rows(c)
        q = qs_ref[rs, :]
        kf = ks_ref[rs, :]
        v = v_ref[rs, :]
        s = lax.dot_general(q, kf.astype(BF16), (((1,), (1,)), ((), ())),
                            preferred_element_type=F32)
        o = jnp.dot((s * decay).astype(BF16), v, preferred_element_type=F32)
        o = o + jnp.dot(q, state.astype(BF16), preferred_element_type=F32) * qdec_f
        acc_ref[rs, :] = o
        return state * cdec_f + kv_outer(kf * kdec_f, v)

    if has_init:
        init_f = s0_ref[0, 0, 0]
        init_b = s0_ref[0, 0, 1]
    else:
        init_f = jnp.zeros((dh, dh), F32)
        init_b = jnp.zeros((dh, dh), F32)
    sout_ref[0, 0, 0] = lax.fori_loop(0, nc, fwd, init_f)

    def bwd(t, state):
        c = nc - 1 - t
        rs = rows(c)
        q = qs_ref[rs, :]
        kf = ks_ref[rs, :]
        v = v_ref[rs, :]
        o = acc_ref[rs, :] + jnp.dot(q, state.astype(BF16), preferred_element_type=F32) * qdec_b
        mu = jnp.mean(o, axis=-1, keepdims=True)
        cen = o - mu
        var = jnp.mean(cen * cen, axis=-1, keepdims=True)
        on = cen * lax.rsqrt(var + NORM_EPS) * gw_ref[...] + gb_ref[...]
        g = gate_ref[rs, :].astype(F32)
        o_ref[rs, :] = (on * (g * _sigmoid(g))).astype(o_ref.dtype)
        return state * cdec_b + kv_outer(kf * kdec_b, v)

    sout_ref[0, 0, 1] = lax.fori_loop(0, nc, bwd, init_b)


def _retention(u, first_row, n, nb, col0, decay_logit, gn_w, gn_b, rope_tabs, init_state):
    n_heads = decay_logit.shape[1]
    dh = gn_w.shape[0] // n_heads
    assert dh == LANES and first_row % n == 0
    chunk = _tile(RET_CHUNK_ROWS, n)
    b0 = first_row // n
    rope = rope_tabs is not None
    has_init = init_state is not None

    def col(kind):
        return lambda b, h: (b + b0, col0 + kind * n_heads + h)

    in_specs = [
        pl.BlockSpec(memory_space=pltpu.SMEM),
        pl.BlockSpec((n, dh), col(0)),
        pl.BlockSpec((n, dh), col(1)),
        pl.BlockSpec((n, dh), col(2)),
        pl.BlockSpec((n, dh), col(3)),
        pl.BlockSpec((1, dh), lambda b, h: (0, h)),
        pl.BlockSpec((1, dh), lambda b, h: (0, h)),
    ]
    args = [decay_logit, u, u, u, u, gn_w.reshape(1, -1), gn_b.reshape(1, -1)]
    if rope:
        in_specs += [pl.BlockSpec((n, dh), lambda b, h: (0, 0))] * 2
        args += list(rope_tabs)
    state_spec = pl.BlockSpec((1, 1, 2, dh, dh), lambda b, h: (b, h, 0, 0, 0))
    if has_init:
        in_specs.append(state_spec)
        args.append(init_state)
    return pl.pallas_call(
        functools.partial(_retention_body, n, chunk, rope, has_init, float(dh) ** -0.5),
        grid=(nb, n_heads),
        in_specs=in_specs,
        out_specs=[pl.BlockSpec((n, dh), lambda b, h: (b, h)), state_spec],
        out_shape=[jax.ShapeDtypeStruct((nb * n, n_heads * dh), BF16),
                   jax.ShapeDtypeStruct((nb, n_heads, 2, dh, dh), F32)],
        scratch_shapes=[pltpu.VMEM((n, dh), BF16), pltpu.VMEM((n, dh), F32),
                        pltpu.VMEM((n, dh), F32)],
        compiler_params=_cparams("parallel", "parallel"), name="retention",
    )(*args)


def _rope_tables(n, dh):
    rows = n // GRID_W
    row = jnp.repeat(jnp.arange(rows, dtype=F32), GRID_W)
    colp = jnp.tile(jnp.arange(GRID_W, dtype=F32), rows)
    n_freq = dh // 4
    inv_freq = ROPE_BASE ** (-jnp.arange(n_freq, dtype=F32) / n_freq)
    ang = jnp.stack([row[:, None] * inv_freq, colp[:, None] * inv_freq], axis=1)
    ang = jnp.broadcast_to(ang[:, :, None, :], (rows * GRID_W, 2, 2, n_freq)).reshape(rows * GRID_W, dh)
    return jnp.cos(ang), jnp.sin(ang)


def _out_proj_body(a1_ref, a2_ref, w_ref, x_ref, g_ref, o_ref):
    k1 = a1_ref.shape[1]
    y = jnp.dot(a1_ref[...], w_ref[pl.ds(0, k1), :], preferred_element_type=F32)
    y = y + jnp.dot(a2_ref[...], w_ref[pl.ds(k1, a2_ref.shape[1]), :], preferred_element_type=F32)
    o_ref[...] = x_ref[...] + g_ref[0] * y


def _out_proj_residual(a1, a2, w, x, gate, geom, first_row):
    n_ctx_rows, seq, nb = geom
    rows, k1 = a1.shape
    k2 = a2.shape[1]
    d = w.shape[1]
    tm = _tile(512, rows, first_row if first_row else rows, n_ctx_rows, seq)
    t0 = first_row // tm
    mrow = _mod_row_fn(tm, first_row, n_ctx_rows, seq, nb)
    return pl.pallas_call(
        _out_proj_body, grid=(rows // tm,),
        in_specs=[
            pl.BlockSpec((tm, k1), lambda i: (i, 0)),
            pl.BlockSpec((tm, k2), lambda i: (i, 0)),
            pl.BlockSpec((k1 + k2, d), lambda i: (0, 0)),
            pl.BlockSpec((tm, d), lambda i: (i + t0, 0)),
            pl.BlockSpec((1, 1, d), lambda i: (mrow(i), 0, 0)),
        ],
        out_specs=pl.BlockSpec((tm, d), lambda i: (i + t0, 0)),
        out_shape=jax.ShapeDtypeStruct(x.shape, F32),
        input_output_aliases={3: 0},
        compiler_params=_cparams("parallel"), name="out_proj",
    )(a1, a2, w, x, gate)


def _swiglu_partial(xb, w1_ref, w3_ref, w2_ref):
    h1 = jnp.dot(xb, w1_ref[0], preferred_element_type=F32)
    h3 = jnp.dot(xb, w3_ref[0], preferred_element_type=F32)
    t = (h1 * _sigmoid(h1) * h3).astype(BF16)
    return jnp.dot(t, w2_ref[0], preferred_element_type=F32)


def _ffn_dense_body(h_ref, w1_ref, w3_ref, w2_ref, x_ref, g_ref, o_ref, acc_ref):
    f = pl.program_id(1)

    @pl.when(f == 0)
    def _():
        acc_ref[...] = jnp.zeros_like(acc_ref)

    acc_ref[...] += _swiglu_partial(h_ref[...], w1_ref, w3_ref, w2_ref)

    @pl.when(f == pl.num_programs(1) - 1)
    def _():
        o_ref[...] = x_ref[...] + g_ref[0] * acc_ref[...]


def _ffn_dense(h, w1, w3, w2, x, gate, geom, first_row):
    n_ctx_rows, seq, nb = geom
    rows, d = h.shape
    fdim = w1.shape[2]
    tm = _tile(512, rows, first_row if first_row else rows, n_ctx_rows, seq)
    tf = _tile(512, fdim)
    t0 = first_row // tm
    mrow = _mod_row_fn(tm, first_row, n_ctx_rows, seq, nb)
    return pl.pallas_call(
        _ffn_dense_body, grid=(rows // tm, fdim // tf),
        in_specs=[
            pl.BlockSpec((tm, d), lambda i, f: (i, 0)),
            pl.BlockSpec((1, d, tf), lambda i, f: (0, 0, f)),
            pl.BlockSpec((1, d, tf), lambda i, f: (0, 0, f)),
            pl.BlockSpec((1, tf, d), lambda i, f: (0, f, 0)),
            pl.BlockSpec((tm, d), lambda i, f: (i + t0, 0)),
            pl.BlockSpec((1, 1, d), lambda i, f: (mrow(i), 0, 0)),
        ],
        out_specs=pl.BlockSpec((tm, d), lambda i, f: (i + t0, 0)),
        out_shape=jax.ShapeDtypeStruct(x.shape, F32),
        scratch_shapes=[pltpu.VMEM((tm, d), F32)],
        input_output_aliases={4: 0},
        compiler_params=_cparams("parallel", "arbitrary"), name="ffn_dense",
    )(h, w1, w3, w2, x, gate)


def _ffn_grouped_body(te_ref, nv_ref, xs_ref, w1_ref, w3_ref, w2_ref, o_ref, xb_ref, acc_ref):
    i = pl.program_id(0)
    f = pl.program_id(1)
    live = i < nv_ref[0]

    @pl.when(f == 0)
    def _():
        acc_ref[...] = jnp.zeros_like(acc_ref)
        xb_ref[...] = xs_ref[...].astype(BF16)

    @pl.when(live)
    def _():
        acc_ref[...] += _swiglu_partial(xb_ref[...], w1_ref, w3_ref, w2_ref)

    @pl.when(f == pl.num_programs(1) - 1)
    def _():
        o_ref[...] = acc_ref[...]


def _ffn_grouped(xs, tile_expert, n_live, w1, w3, w2, tm):
    rows, d = xs.shape
    fdim = w1.shape[2]
    tf = _tile(512, fdim)
    nf = fdim // tf

    def wcol(i, f, te, nv):
        return jnp.where(i < nv[0], f, nf - 1)

    grid_spec = pltpu.PrefetchScalarGridSpec(
        num_scalar_prefetch=2, grid=(rows // tm, nf),
        in_specs=[
            pl.BlockSpec((tm, d), lambda i, f, te, nv: (i, 0)),
            pl.BlockSpec((1, d, tf), lambda i, f, te, nv: (te[i], 0, wcol(i, f, te, nv))),
            pl.BlockSpec((1, d, tf), lambda i, f, te, nv: (te[i], 0, wcol(i, f, te, nv))),
            pl.BlockSpec((1, tf, d), lambda i, f, te, nv: (te[i], wcol(i, f, te, nv), 0)),
        ],
        out_specs=pl.BlockSpec((tm, d), lambda i, f, te, nv: (i, 0)),
        scratch_shapes=[pltpu.VMEM((tm, d), BF16), pltpu.VMEM((tm, d), F32)],
    )
    return pl.pallas_call(
        _ffn_grouped_body, grid_spec=grid_spec,
        out_shape=jax.ShapeDtypeStruct((rows, d), F32),
        compiler_params=_cparams("parallel", "arbitrary"), name="ffn_grouped",
    )(tile_expert, n_live, xs, w1, w3, w2)


def _gather_rows_body(tm, src_ref, h_hbm, o_ref, sem):
    def issue(j, carry):
        pltpu.make_async_copy(h_hbm.at[pl.ds(src_ref[0, 0, j], 1), :],
                              o_ref.at[pl.ds(j, 1), :], sem).start()
        return carry

    lax.fori_loop(0, tm, issue, 0)
    pltpu.make_async_copy(h_hbm.at[pl.ds(0, tm), :], o_ref, sem).wait()


def _gather_rows(h, src, tm):
    n_tiles = src.shape[0]
    d = h.shape[1]
    return pl.pallas_call(
        functools.partial(_gather_rows_body, tm), grid=(n_tiles,),
        in_specs=[pl.BlockSpec((1, 1, tm), lambda i: (i, 0, 0), memory_space=pltpu.SMEM),
                  pl.BlockSpec(memory_space=pl.ANY)],
        out_specs=pl.BlockSpec((tm, d), lambda i: (i, 0)),
        out_shape=jax.ShapeDtypeStruct((n_tiles * tm, d), h.dtype),
        scratch_shapes=[pltpu.SemaphoreType.DMA(())],
        compiler_params=_cparams("arbitrary"), name="moe_dispatch",
    )(src, h)


def _combine_body(tt, pos_ref, y_hbm, wgt_ref, x_ref, g_ref, o_ref, buf_ref, sem):
    def issue(j, carry):
        for k in range(TOP_K):
            pltpu.make_async_copy(y_hbm.at[pl.ds(pos_ref[0, 0, TOP_K * j + k], 1), :],
                                  buf_ref.at[k, pl.ds(j, 1), :], sem).start()
        return carry

    lax.fori_loop(0, tt, issue, 0)
    for k in range(TOP_K):
        pltpu.make_async_copy(y_hbm.at[pl.ds(0, tt), :], buf_ref.at[k], sem).wait()
    w = wgt_ref[...]
    lane = lax.broadcasted_iota(I32, w.shape, 1)
    y = jnp.zeros(x_ref.shape, F32)
    for k in range(TOP_K):
        wk = jnp.sum(jnp.where(lane == k, w, 0.0), axis=-1, keepdims=True)
        y = y + wk * buf_ref[k]
    o_ref[...] = x_ref[...] + g_ref[0] * y


def _combine_residual(y_sorted, pos, wgt, x, gate, geom, first_row):
    n_ctx_rows, seq, nb = geom
    tokens = wgt.shape[0]
    d = x.shape[1]
    tt = pos.shape[2] // TOP_K
    t0 = first_row // tt
    mrow = _mod_row_fn(tt, first_row, n_ctx_rows, seq, nb)
    return pl.pallas_call(
        functools.partial(_combine_body, tt), grid=(tokens // tt,),
        in_specs=[
            pl.BlockSpec((1, 1, TOP_K * tt), lambda i: (i, 0, 0), memory_space=pltpu.SMEM),
            pl.BlockSpec(memory_space=pl.ANY),
            pl.BlockSpec((tt, LANES), lambda i: (i, 0)),
            pl.BlockSpec((tt, d), lambda i: (i + t0, 0)),
            pl.BlockSpec((1, 1, d), lambda i: (mrow(i), 0, 0)),
        ],
        out_specs=pl.BlockSpec((tt, d), lambda i: (i + t0, 0)),
        out_shape=jax.ShapeDtypeStruct(x.shape, F32),
        scratch_shapes=[pltpu.VMEM((TOP_K, tt, d), F32), pltpu.SemaphoreType.DMA(())],
        input_output_aliases={3: 0},
        compiler_params=_cparams("arbitrary"), name="moe_combine",
    )(pos, y_sorted, wgt, x, gate)


def _moe(h, idx, wgt, w1, w3, w2, x, gate, geom, first_row):
    tokens, d = h.shape
    n_exp = w1.shape[0]
    tm = _tile(512, tokens)
    tt = _tile(256, tokens, first_row if first_row else tokens, geom[0], geom[1])
    n_assign = TOP_K * tokens
    n_tiles = n_assign // tm + n_exp
    e_flat = idx[:, :TOP_K].reshape(n_assign)
    onehot = (e_flat[:, None] == jnp.arange(n_exp, dtype=I32)[None, :]).astype(I32)
    csum = jnp.cumsum(onehot, axis=0)
    rank = jnp.sum(csum * onehot, axis=1) - 1
    counts = csum[-1]
    padded = ((counts + tm - 1) // tm) * tm
    ends = jnp.cumsum(padded)
    starts = ends - padded
    pos = jnp.sum(onehot * starts[None, :], axis=1) + rank
    n_live = (ends[-1] // tm).astype(I32).reshape(1)
    tile_start = jnp.arange(n_tiles, dtype=I32) * tm
    tile_expert = jnp.sum((tile_start[:, None] >= ends[None, :]).astype(I32), axis=1)
    last_expert = jnp.max(jnp.where(counts > 0, jnp.arange(n_exp, dtype=I32), 0))
    tile_expert = jnp.minimum(tile_expert, last_expert).astype(I32)
    src = jnp.zeros((n_tiles * tm,), I32).at[pos].set(jnp.arange(n_assign, dtype=I32) // TOP_K)

    xs = _gather_rows(h, src.reshape(n_tiles, 1, tm), tm)
    ys = _ffn_grouped(xs, tile_expert, n_live, w1, w3, w2, tm)
    return _combine_residual(ys, pos.reshape(tokens // tt, 1, TOP_K * tt), wgt, x, gate, geom, first_row)


def _final_norm_body(x_ref, w_ref, o_ref):
    x = x_ref[...]
    ms = jnp.mean(x * x, axis=-1, keepdims=True)
    o_ref[...] = x * lax.rsqrt(ms + NORM_EPS) * w_ref[...]


def _final_norm(x, w, first_row):
    r, d = x.shape
    rows = r - first_row
    tm = _tile(256, rows, first_row)
    t0 = first_row // tm
    return pl.pallas_call(
        _final_norm_body, grid=(rows // tm,),
        in_specs=[pl.BlockSpec((tm, d), lambda i: (i + t0, 0)),
                  pl.BlockSpec((1, d), lambda i: (0, 0))],
        out_specs=pl.BlockSpec((tm, d), lambda i: (i, 0)),
        out_shape=jax.ShapeDtypeStruct((rows, d), F32),
        compiler_params=_cparams("parallel"), name="final_norm",
    )(x, w.reshape(1, d))


def kernel(x, c, ctx, c_ctx, w_mod, b_mod, norm1_w, norm2_w, w_in, conv_w, conv_b, conv_ln_w,
           conv_ln_b, ret_decay_logit, ret_gn_w, ret_gn_b, w_out, ffn_w1, ffn_w3, ffn_w2,
           moe_router, moe_router_b, moe_w1, moe_w3, moe_w2, final_w):
    nb, seq, d = x.shape
    n_ctx = ctx.shape[1]
    depth = w_mod.shape[0]
    cw = conv_w.shape[2]
    n_heads = ret_decay_logit.shape[2]
    dh = ret_gn_w.shape[1] // n_heads
    n_ctx_rows = nb * n_ctx
    geom = (n_ctx_rows, seq, nb)
    q_col0 = 2 * cw // LANES

    xs = jnp.concatenate([ctx.reshape(n_ctx_rows, d), x.reshape(nb * seq, d)], axis=0)

    rm = -(-(nb + 1) // SUBLANES) * SUBLANES
    c_all = jnp.zeros((rm, d), F32).at[:nb].set(c).at[nb].set(c_ctx)
    mods = _modulation(c_all, w_mod, b_mod).reshape(depth, rm, 6, 1, d)

    w_in_b = w_in.astype(BF16)
    w_out_b = w_out.astype(BF16)
    ffn_w1_b, ffn_w3_b, ffn_w2_b = (w.astype(BF16) for w in (ffn_w1, ffn_w3, ffn_w2))
    moe_w1_b, moe_w3_b, moe_w2_b = (w.astype(BF16) for w in (moe_w1, moe_w3, moe_w2))
    rope_tabs = _rope_tables(seq, dh)

    for i in range(depth):
        last = i == depth - 1
        j = i // 2
        sh1, sc1, g1, sh2, sc2, g2 = (mods[i, :, m] for m in range(6))

        h = _norm_mod(xs, norm1_w[i], sc1, sh1, geom, 0, BF16)
        u = _matmul(h, w_in_b[i], BF16)

        ret_ctx, ctx_state = _retention(u, 0, n_ctx, nb, q_col0, ret_decay_logit[i],
                                        ret_gn_w[i], ret_gn_b[i], None, None)
        ret_lat, _ = _retention(u, n_ctx_rows, seq, nb, q_col0, ret_decay_logit[i],
                                ret_gn_w[i], ret_gn_b[i], rope_tabs, ctx_state)
        conv_lat = _conformer_conv(u, n_ctx_rows, seq, nb, conv_w[i], conv_b[i],
                                   conv_ln_w[i], conv_ln_b[i])
        xs = _out_proj_residual(conv_lat, ret_lat, w_out_b[i], xs, g1, geom, n_ctx_rows)
        if not last:
            conv_ctx = _conformer_conv(u, 0, n_ctx, nb, conv_w[i], conv_b[i],
                                       conv_ln_w[i], conv_ln_b[i])
            xs = _out_proj_residual(conv_ctx, ret_ctx, w_out_b[i], xs, g1, geom, 0)

        first_row = n_ctx_rows if last else 0
        if i % 2 == 0:
            h2 = _norm_mod(xs, norm2_w[i], sc2, sh2, geom, first_row, BF16)
            xs = _ffn_dense(h2, ffn_w1_b[j:j + 1], ffn_w3_b[j:j + 1], ffn_w2_b[j:j + 1],
                            xs, g2, geom, first_row)
        else:
            h2, idx, wgt = _norm_mod(xs, norm2_w[i], sc2, sh2, geom, first_row, F32,
                                     router=(moe_router[j], moe_router_b[j]))
            xs = _moe(h2, idx, wgt, moe_w1_b[j], moe_w3_b[j], moe_w2_b[j], xs, g2, geom, first_row)

    return _final_norm(xs, final_w, n_ctx_rows).reshape(nb, seq, d)
```

```python
import functools

import jax
import jax.numpy as jnp
from jax import lax
from jax.experimental import pallas as pl
from jax.experimental.pallas import tpu as pltpu

F32 = jnp.float32
BF16 = jnp.bfloat16
I32 = jnp.int32

GRID_W = 64
ROPE_BASE = 10000.0
NORM_EPS = 1e-6
TOP_K = 2
LANES = 128
SUBLANES = 8
VMEM_LIMIT_BYTES = 56 * 1024 * 1024


def _cparams(*sem):
    return pltpu.CompilerParams(dimension_semantics=sem, vmem_limit_bytes=VMEM_LIMIT_BYTES)


def _tile(pref, *dims):
    t = pref
    while any(d % t for d in dims):
        t //= 2
    return t


def _sigmoid(v):
    return 1.0 / (1.0 + jnp.exp(-v))


def _log_sigmoid(v):
    return jnp.minimum(v, 0.0) - jnp.log1p(jnp.exp(-jnp.abs(v)))


def _mod_body(c_ref, w_ref, b_ref, o_ref):
    c = c_ref[...]
    s = (c * _sigmoid(c)).astype(BF16)
    o_ref[0] = jnp.dot(s, w_ref[0].astype(BF16), preferred_element_type=F32) + b_ref[0]


def _modulation(c_all, w_mod, b_mod):
    depth, d, n6 = w_mod.shape
    rm = c_all.shape[0]
    tn = _tile(1024, n6)
    return pl.pallas_call(
        _mod_body,
        grid=(depth, n6 // tn),
        in_specs=[
            pl.BlockSpec((rm, d), lambda l, j: (0, 0)),
            pl.BlockSpec((1, d, tn), lambda l, j: (l, 0, j)),
            pl.BlockSpec((1, 1, tn), lambda l, j: (l, 0, j)),
        ],
        out_specs=pl.BlockSpec((1, rm, tn), lambda l, j: (l, 0, j)),
        out_shape=jax.ShapeDtypeStruct((depth, rm, n6), F32),
        compiler_params=_cparams("parallel", "parallel"),
        name="modulation",
    )(c_all, w_mod, b_mod.reshape(depth, 1, n6))


def _mod_row_fn(tm, first_row, n_ctx_rows, seq, nb):
    t0 = first_row // tm
    n_ctx_tiles = n_ctx_rows // tm
    per_seq = seq // tm

    def f(i):
        g = i + t0
        return jnp.where(g < n_ctx_tiles, nb, (g - n_ctx_tiles) // per_seq)

    return f


def _modulated_norm(x, nw, sc, sh):
    ms = jnp.mean(x * x, axis=-1, keepdims=True)
    y = x * lax.rsqrt(ms + NORM_EPS) * nw
    return y * (1.0 + sc) + sh


def _norm_mod_body(x_ref, nw_ref, sc_ref, sh_ref, h_ref):
    h = _modulated_norm(x_ref[...], nw_ref[...], sc_ref[0], sh_ref[0])
    h_ref[...] = h.astype(h_ref.dtype)


def _router_top2(h, wr, br, n_exp):
    logits = jnp.dot(h, wr, preferred_element_type=F32, precision=lax.Precision.HIGHEST) + br
    lane = lax.broadcasted_iota(I32, logits.shape, 1)
    neg = jnp.float32(-jnp.inf)
    logits = jnp.where(lane < n_exp, logits, neg)
    m1 = jnp.max(logits, axis=-1, keepdims=True)
    i1 = jnp.min(jnp.where(logits == m1, lane, LANES), axis=-1, keepdims=True)
    rest = jnp.where(lane == i1, neg, logits)
    m2 = jnp.max(rest, axis=-1, keepdims=True)
    i2 = jnp.min(jnp.where(rest == m2, lane, LANES), axis=-1, keepdims=True)
    e = jnp.exp(m2 - m1)
    w1 = 1.0 / (1.0 + e)
    w2 = e / (1.0 + e)
    idx = jnp.where(lane == 0, i1, jnp.where(lane == 1, i2, 0))
    wgt = jnp.where(lane == 0, w1, jnp.where(lane == 1, w2, 0.0))
    return idx, wgt


def _norm_mod_router_body(n_exp, x_ref, nw_ref, sc_ref, sh_ref, wr_ref, br_ref,
                          h_ref, idx_ref, wgt_ref):
    h = _modulated_norm(x_ref[...], nw_ref[...], sc_ref[0], sh_ref[0])
    h_ref[...] = h.astype(h_ref.dtype)
    idx, wgt = _router_top2(h, wr_ref[...], br_ref[...], n_exp)
    idx_ref[...] = idx
    wgt_ref[...] = wgt


def _norm_mod(x, nw, sc, sh, geom, first_row, out_dtype, router=None):
    n_ctx_rows, seq, nb = geom
    r, d = x.shape
    rows = r - first_row
    tm = _tile(256, rows, first_row if first_row else rows, n_ctx_rows, seq)
    t0 = first_row // tm
    mrow = _mod_row_fn(tm, first_row, n_ctx_rows, seq, nb)
    in_specs = [
        pl.BlockSpec((tm, d), lambda i: (i + t0, 0)),
        pl.BlockSpec((1, d), lambda i: (0, 0)),
        pl.BlockSpec((1, 1, d), lambda i: (mrow(i), 0, 0)),
        pl.BlockSpec((1, 1, d), lambda i: (mrow(i), 0, 0)),
    ]
    args = [x, nw.reshape(1, d), sc, sh]
    row_spec = pl.BlockSpec((tm, d), lambda i: (i, 0))
    if router is None:
        return pl.pallas_call(
            _norm_mod_body, grid=(rows // tm,), in_specs=in_specs, out_specs=row_spec,
            out_shape=jax.ShapeDtypeStruct((rows, d), out_dtype),
            compiler_params=_cparams("parallel"), name="norm_mod",
        )(*args)
    wr, br = router
    n_exp = wr.shape[1]
    wr_p = jnp.zeros((d, LANES), F32).at[:, :n_exp].set(wr)
    br_p = jnp.zeros((1, LANES), F32).at[0, :n_exp].set(br)
    in_specs += [pl.BlockSpec((d, LANES), lambda i: (0, 0)),
                 pl.BlockSpec((1, LANES), lambda i: (0, 0))]
    lane_spec = pl.BlockSpec((tm, LANES), lambda i: (i, 0))
    return pl.pallas_call(
        functools.partial(_norm_mod_router_body, n_exp),
        grid=(rows // tm,), in_specs=in_specs,
        out_specs=[row_spec, lane_spec, lane_spec],
        out_shape=[jax.ShapeDtypeStruct((rows, d), out_dtype),
                   jax.ShapeDtypeStruct((rows, LANES), I32),
                   jax.ShapeDtypeStruct((rows, LANES), F32)],
        compiler_params=_cparams("parallel"), name="norm_mod_router",
    )(*args, wr_p, br_p)


def _matmul_body(a_ref, w_ref, o_ref):
    o_ref[...] = jnp.dot(a_ref[...], w_ref[...], preferred_element_type=F32).astype(o_ref.dtype)


def _matmul(a, w, out_dtype):
    m, k = a.shape
    n = w.shape[1]
    tm = _tile(1024, m)
    tn = _tile(1024, n)
    return pl.pallas_call(
        _matmul_body, grid=(m // tm, n // tn),
        in_specs=[pl.BlockSpec((tm, k), lambda i, j: (i, 0)),
                  pl.BlockSpec((k, tn), lambda i, j: (0, j))],
        out_specs=pl.BlockSpec((tm, tn), lambda i, j: (i, j)),
        out_shape=jax.ShapeDtypeStruct((m, n), out_dtype),
        compiler_params=_cparams("parallel", "parallel"), name="in_proj",
    )(a, w)


CONV_ROWS = 64
CONV_PAD = 16


def _conv_body(n, kw, a_ref, b_ref, cw_ref, cb_ref, lw_ref, lb_ref, o_ref, g_ref, acc_ref):
    cw = a_ref.shape[1]
    half = kw // 2
    zeros = jnp.zeros((CONV_PAD, cw), F32)
    g_ref[pl.ds(0, CONV_PAD), :] = zeros
    g_ref[pl.ds(CONV_PAD + n, CONV_PAD), :] = zeros

    def glu_step(i, carry):
        r = pl.multiple_of(i * CONV_ROWS, CONV_ROWS)
        a = a_ref[pl.ds(r, CONV_ROWS), :].astype(F32)
        b = b_ref[pl.ds(r, CONV_ROWS), :].astype(F32)
        g_ref[pl.ds(r + CONV_PAD, CONV_ROWS), :] = a * _sigmoid(b)
        return carry

    lax.fori_loop(0, n // CONV_ROWS, glu_step, 0)

    def conv_step(i, carry):
        r = pl.multiple_of(i * CONV_ROWS, CONV_ROWS)
        for c in range(cw // LANES):
            lanes = pl.ds(c * LANES, LANES)
            win = g_ref[pl.ds(r, CONV_ROWS + 2 * CONV_PAD), lanes]
            out = jnp.zeros((CONV_ROWS, LANES), F32) + cb_ref[:, lanes]
            for s in range(SUBLANES):
                part = None
                for q in range(2 * CONV_PAD // SUBLANES):
                    k = SUBLANES * q + s - (CONV_PAD - half)
                    if 0 <= k < kw:
                        term = (win[SUBLANES * q:SUBLANES * q + CONV_ROWS + SUBLANES, :]
                                * cw_ref[pl.ds(k, 1), lanes])
                        part = term if part is None else part + term
                out = out + part[s:s + CONV_ROWS, :]
            acc_ref[:, lanes] = out
        acc = acc_ref[...]
        mu = jnp.mean(acc, axis=-1, keepdims=True)
        cen = acc - mu
        var = jnp.mean(cen * cen, axis=-1, keepdims=True)
        y = cen * lax.rsqrt(var + NORM_EPS) * lw_ref[...] + lb_ref[...]
        o_ref[pl.ds(r, CONV_ROWS), :] = (y * _sigmoid(y)).astype(o_ref.dtype)
        return carry

    lax.fori_loop(0, n // CONV_ROWS, conv_step, 0)


def _conformer_conv(u, first_row, n, nb, conv_w, conv_b, ln_w, ln_b):
    kw, cw = conv_w.shape
    assert kw // 2 < CONV_PAD and n % CONV_ROWS == 0 and first_row % n == 0
    b0 = first_row // n
    return pl.pallas_call(
        functools.partial(_conv_body, n, kw),
        grid=(nb,),
        in_specs=[
            pl.BlockSpec((n, cw), lambda b: (b + b0, 0)),
            pl.BlockSpec((n, cw), lambda b: (b + b0, 1)),
            pl.BlockSpec((kw, cw), lambda b: (0, 0)),
            pl.BlockSpec((1, cw), lambda b: (0, 0)),
            pl.BlockSpec((1, cw), lambda b: (0, 0)),
            pl.BlockSpec((1, cw), lambda b: (0, 0)),
        ],
        out_specs=pl.BlockSpec((n, cw), lambda b: (b, 0)),
        out_shape=jax.ShapeDtypeStruct((nb * n, cw), BF16),
        scratch_shapes=[pltpu.VMEM((n + 2 * CONV_PAD, cw), F32), pltpu.VMEM((CONV_ROWS, cw), F32)],
        compiler_params=_cparams("parallel"), name="conformer_conv",
    )(u, u, conv_w, conv_b.reshape(1, cw), ln_w.reshape(1, cw), ln_b.reshape(1, cw))


RET_CHUNK_ROWS = 256
RET_UNROLL = 4


def _retention_body(n, chunk, rope, has_init, k_scale, *refs):
    dl_ref, q_ref, k_ref, v_ref, gate_ref, gw_ref, gb_ref = refs[:7]
    pos = 7
    if rope:
        cos_ref, sin_ref = refs[pos:pos + 2]
        pos += 2
    if has_init:
        s0_ref = refs[pos]
        pos += 1
    o_ref, sout_ref, qs_ref, acc_ref, u_ref, st_ref, decay_ref, vec_ref, cd_ref = refs[pos:pos + 9]
    dh = q_ref.shape[1]
    nc = n // chunk
    h = pl.program_id(0)
    nt = (((1,), (1,)), ((), ()))
    tn = (((0,), (0,)), ((), ()))

    def rows(c):
        return pl.ds(pl.multiple_of(c * chunk, chunk), chunk)

    @pl.when(pl.program_id(1) == 0)
    def _():
        def log_decay(direction, shape):
            return _log_sigmoid(jnp.full(shape, dl_ref[direction, h], F32))

        ii = lax.broadcasted_iota(I32, (chunk, chunk), 0)
        jj = lax.broadcasted_iota(I32, (chunk, chunk), 1)
        diff = (ii - jj).astype(F32)
        decay_ref[...] = jnp.where(
            diff >= 0.0,
            jnp.exp(jnp.maximum(diff, 0.0) * log_decay(0, (chunk, chunk))),
            jnp.exp(jnp.maximum(-diff, 0.0) * log_decay(1, (chunk, chunk))))
        idx = lax.broadcasted_iota(I32, (chunk, dh), 0).astype(F32)
        lg_f = log_decay(0, (chunk, dh))
        lg_b = log_decay(1, (chunk, dh))
        vec_ref[0] = jnp.exp((idx + 1.0) * lg_f)
        vec_ref[1] = jnp.exp((chunk - 1.0 - idx) * lg_f)
        vec_ref[2] = jnp.exp((chunk - idx) * lg_b)
        vec_ref[3] = jnp.exp(idx * lg_b)
        cd_ref[0] = jnp.exp(chunk * log_decay(0, (dh, dh)))
        cd_ref[1] = jnp.exp(chunk * log_decay(1, (dh, dh)))

    def intra(c, carry):
        rs = rows(c)
        q = q_ref[rs, :].astype(F32)
        k = k_ref[rs, :].astype(F32)
        if rope:
            lane = lax.broadcasted_iota(I32, (chunk, dh), 1)
            first = (lane % (dh // 2)) < (dh // 4)
            cos = cos_ref[rs, :]
            sin = sin_ref[rs, :]

            def rot(t):
                return jnp.where(first, -pltpu.roll(t, dh - dh // 4, 1), pltpu.roll(t, dh // 4, 1))

            q = q * cos + rot(q) * sin
            k = k * cos + rot(k) * sin
        qb = q.astype(BF16)
        qs_ref[rs, :] = qb
        kf = k * k_scale
        v = v_ref[rs, :]
        s = lax.dot_general(qb, kf.astype(BF16), nt, preferred_element_type=F32)
        acc_ref[rs, :] = jnp.dot((s * decay_ref[...]).astype(BF16), v, preferred_element_type=F32)
        kd = jnp.concatenate([kf * vec_ref[1], kf * vec_ref[3]], axis=1).astype(BF16)
        u_ref[c] = lax.dot_general(kd, v, tn, preferred_element_type=F32)
        return carry

    lax.fori_loop(0, nc, intra, 0, unroll=min(nc, RET_UNROLL))

    if has_init:
        init_f = s0_ref[0, 0, 0]
        init_b = s0_ref[0, 0, 1]
    else:
        init_f = jnp.zeros((dh, dh), F32)
        init_b = jnp.zeros((dh, dh), F32)

    def scan_f(c, state):
        st_ref[c, :, pl.ds(0, dh)] = state.astype(BF16)
        return state * cd_ref[0] + u_ref[c, pl.ds(0, dh), :]

    def scan_b(t, state):
        c = nc - 1 - t
        st_ref[c, :, pl.ds(dh, dh)] = state.astype(BF16)
        return state * cd_ref[1] + u_ref[c, pl.ds(dh, dh), :]

    sout_ref[0, 0, 0] = lax.fori_loop(0, nc, scan_f, init_f)
    sout_ref[0, 0, 1] = lax.fori_loop(0, nc, scan_b, init_b)

    def readout(c, carry):
        rs = rows(c)
        r = jnp.dot(qs_ref[rs, :], st_ref[c], preferred_element_type=F32)
        o = acc_ref[rs, :] + r[:, :dh] * vec_ref[0] + r[:, dh:] * vec_ref[2]
        mu = jnp.mean(o, axis=-1, keepdims=True)
        cen = o - mu
        var = jnp.mean(cen * cen, axis=-1, keepdims=True)
        on = cen * lax.rsqrt(var + NORM_EPS) * gw_ref[...] + gb_ref[...]
        g = gate_ref[rs, :].astype(F32)
        o_ref[rs, :] = (on * (g * _sigmoid(g))).astype(o_ref.dtype)
        return carry

    lax.fori_loop(0, nc, readout, 0, unroll=min(nc, RET_UNROLL))


def _retention(u, first_row, n, nb, col0, decay_logit, gn_w, gn_b, rope_tabs, init_state):
    n_heads = decay_logit.shape[1]
    dh = gn_w.shape[0] // n_heads
    assert dh == LANES and first_row % n == 0
    chunk = _tile(RET_CHUNK_ROWS, n)
    nc = n // chunk
    b0 = first_row // n
    rope = rope_tabs is not None
    has_init = init_state is not None

    def col(kind):
        return lambda h, b: (b + b0, col0 + kind * n_heads + h)

    in_specs = [
        pl.BlockSpec(memory_space=pltpu.SMEM),
        pl.BlockSpec((n, dh), col(0)),
        pl.BlockSpec((n, dh), col(1)),
        pl.BlockSpec((n, dh), col(2)),
        pl.BlockSpec((n, dh), col(3)),
        pl.BlockSpec((1, dh), lambda h, b: (0, h)),
        pl.BlockSpec((1, dh), lambda h, b: (0, h)),
    ]
    args = [decay_logit, u, u, u, u, gn_w.reshape(1, -1), gn_b.reshape(1, -1)]
    if rope:
        in_specs += [pl.BlockSpec((n, dh), lambda h, b: (0, 0))] * 2
        args += list(rope_tabs)
    state_spec = pl.BlockSpec((1, 1, 2, dh, dh), lambda h, b: (b, h, 0, 0, 0))
    if has_init:
        in_specs.append(state_spec)
        args.append(init_state)
    return pl.pallas_call(
        functools.partial(_retention_body, n, chunk, rope, has_init, float(dh) ** -0.5),
        grid=(n_heads, nb),
        in_specs=in_specs,
        out_specs=[pl.BlockSpec((n, dh), lambda h, b: (b, h)), state_spec],
        out_shape=[jax.ShapeDtypeStruct((nb * n, n_heads * dh), BF16),
                   jax.ShapeDtypeStruct((nb, n_heads, 2, dh, dh), F32)],
        scratch_shapes=[
            pltpu.VMEM((n, dh), BF16),
            pltpu.VMEM((n, dh), F32),
            pltpu.VMEM((nc, 2 * dh, dh), F32),
            pltpu.VMEM((nc, dh, 2 * dh), BF16),
            pltpu.VMEM((chunk, chunk), F32),
            pltpu.VMEM((4, chunk, dh), F32),
            pltpu.VMEM((2, dh, dh), F32),
        ],
        compiler_params=_cparams("arbitrary", "arbitrary"), name="retention",
    )(*args)


def _rope_tables(n, dh):
    rows = n // GRID_W
    row = jnp.repeat(jnp.arange(rows, dtype=F32), GRID_W)
    colp = jnp.tile(jnp.arange(GRID_W, dtype=F32), rows)
    n_freq = dh // 4
    inv_freq = ROPE_BASE ** (-jnp.arange(n_freq, dtype=F32) / n_freq)
    ang = jnp.stack([row[:, None] * inv_freq, colp[:, None] * inv_freq], axis=1)
    ang = jnp.broadcast_to(ang[:, :, None, :], (rows * GRID_W, 2, 2, n_freq)).reshape(rows * GRID_W, dh)
    return jnp.cos(ang), jnp.sin(ang)


def _out_proj_body(a1_ref, a2_ref, w_ref, x_ref, g_ref, o_ref):
    k1 = a1_ref.shape[1]
    y = jnp.dot(a1_ref[...], w_ref[pl.ds(0, k1), :], preferred_element_type=F32)
    y = y + jnp.dot(a2_ref[...], w_ref[pl.ds(k1, a2_ref.shape[1]), :], preferred_element_type=F32)
    o_ref[...] = x_ref[...] + g_ref[0] * y


def _out_proj_residual(a1, a2, w, x, gate, geom, first_row):
    n_ctx_rows, seq, nb = geom
    rows, k1 = a1.shape
    k2 = a2.shape[1]
    d = w.shape[1]
    tm = _tile(512, rows, first_row if first_row else rows, n_ctx_rows, seq)
    t0 = first_row // tm
    mrow = _mod_row_fn(tm, first_row, n_ctx_rows, seq, nb)
    return pl.pallas_call(
        _out_proj_body, grid=(rows // tm,),
        in_specs=[
            pl.BlockSpec((tm, k1), lambda i: (i, 0)),
            pl.BlockSpec((tm, k2), lambda i: (i, 0)),
            pl.BlockSpec((k1 + k2, d), lambda i: (0, 0)),
            pl.BlockSpec((tm, d), lambda i: (i + t0, 0)),
            pl.BlockSpec((1, 1, d), lambda i: (mrow(i), 0, 0)),
        ],
        out_specs=pl.BlockSpec((tm, d), lambda i: (i + t0, 0)),
        out_shape=jax.ShapeDtypeStruct(x.shape, F32),
        input_output_aliases={3: 0},
        compiler_params=_cparams("parallel"), name="out_proj",
    )(a1, a2, w, x, gate)


def _swiglu_partial(xb, w1_ref, w3_ref, w2_ref):
    h1 = jnp.dot(xb, w1_ref[0], preferred_element_type=F32)
    h3 = jnp.dot(xb, w3_ref[0], preferred_element_type=F32)
    t = (h1 * _sigmoid(h1) * h3).astype(BF16)
    return jnp.dot(t, w2_ref[0], preferred_element_type=F32)


def _ffn_dense_body(h_ref, w1_ref, w3_ref, w2_ref, x_ref, g_ref, o_ref, acc_ref):
    f = pl.program_id(1)

    @pl.when(f == 0)
    def _():
        acc_ref[...] = jnp.zeros_like(acc_ref)

    acc_ref[...] += _swiglu_partial(h_ref[...], w1_ref, w3_ref, w2_ref)

    @pl.when(f == pl.num_programs(1) - 1)
    def _():
        o_ref[...] = x_ref[...] + g_ref[0] * acc_ref[...]


def _ffn_dense(h, w1, w3, w2, x, gate, geom, first_row):
    n_ctx_rows, seq, nb = geom
    rows, d = h.shape
    fdim = w1.shape[2]
    tm = _tile(512, rows, first_row if first_row else rows, n_ctx_rows, seq)
    tf = _tile(512, fdim)
    t0 = first_row // tm
    mrow = _mod_row_fn(tm, first_row, n_ctx_rows, seq, nb)
    return pl.pallas_call(
        _ffn_dense_body, grid=(rows // tm, fdim // tf),
        in_specs=[
            pl.BlockSpec((tm, d), lambda i, f: (i, 0)),
            pl.BlockSpec((1, d, tf), lambda i, f: (0, 0, f)),
            pl.BlockSpec((1, d, tf), lambda i, f: (0, 0, f)),
            pl.BlockSpec((1, tf, d), lambda i, f: (0, f, 0)),
            pl.BlockSpec((tm, d), lambda i, f: (i + t0, 0)),
            pl.BlockSpec((1, 1, d), lambda i, f: (mrow(i), 0, 0)),
        ],
        out_specs=pl.BlockSpec((tm, d), lambda i, f: (i + t0, 0)),
        out_shape=jax.ShapeDtypeStruct(x.shape, F32),
        scratch_shapes=[pltpu.VMEM((tm, d), F32)],
        input_output_aliases={4: 0},
        compiler_params=_cparams("parallel", "arbitrary"), name="ffn_dense",
    )(h, w1, w3, w2, x, gate)


def _ffn_grouped_body(te_ref, nv_ref, xs_ref, w1_ref, w3_ref, w2_ref, o_ref, xb_ref, acc_ref):
    i = pl.program_id(0)
    f = pl.program_id(1)
    live = i < nv_ref[0]

    @pl.when(f == 0)
    def _():
        acc_ref[...] = jnp.zeros_like(acc_ref)
        xb_ref[...] = xs_ref[...].astype(BF16)

    @pl.when(live)
    def _():
        acc_ref[...] += _swiglu_partial(xb_ref[...], w1_ref, w3_ref, w2_ref)

    @pl.when(f == pl.num_programs(1) - 1)
    def _():
        o_ref[...] = acc_ref[...]


def _ffn_grouped(xs, tile_expert, n_live, w1, w3, w2, tm):
    rows, d = xs.shape
    fdim = w1.shape[2]
    tf = _tile(512, fdim)
    nf = fdim // tf

    def wcol(i, f, te, nv):
        return jnp.where(i < nv[0], f, nf - 1)

    grid_spec = pltpu.PrefetchScalarGridSpec(
        num_scalar_prefetch=2, grid=(rows // tm, nf),
        in_specs=[
            pl.BlockSpec((tm, d), lambda i, f, te, nv: (i, 0)),
            pl.BlockSpec((1, d, tf), lambda i, f, te, nv: (te[i], 0, wcol(i, f, te, nv))),
            pl.BlockSpec((1, d, tf), lambda i, f, te, nv: (te[i], 0, wcol(i, f, te, nv))),
            pl.BlockSpec((1, tf, d), lambda i, f, te, nv: (te[i], wcol(i, f, te, nv), 0)),
        ],
        out_specs=pl.BlockSpec((tm, d), lambda i, f, te, nv: (i, 0)),
        scratch_shapes=[pltpu.VMEM((tm, d), BF16), pltpu.VMEM((tm, d), F32)],
    )
    return pl.pallas_call(
        _ffn_grouped_body, grid_spec=grid_spec,
        out_shape=jax.ShapeDtypeStruct((rows, d), F32),
        compiler_params=_cparams("parallel", "arbitrary"), name="ffn_grouped",
    )(tile_expert, n_live, xs, w1, w3, w2)


DMA_ISSUE_UNROLL = 8


def _gather_rows_body(tm, src_ref, h_hbm, o_ref, sem):
    def issue(j, carry):
        pltpu.make_async_copy(h_hbm.at[pl.ds(src_ref[0, 0, j], 1), :],
                              o_ref.at[pl.ds(j, 1), :], sem).start()
        return carry

    lax.fori_loop(0, tm, issue, 0, unroll=DMA_ISSUE_UNROLL)
    pltpu.make_async_copy(h_hbm.at[pl.ds(0, tm), :], o_ref, sem).wait()


def _gather_rows(h, src, tm):
    n_tiles = src.shape[0]
    d = h.shape[1]
    return pl.pallas_call(
        functools.partial(_gather_rows_body, tm), grid=(n_tiles,),
        in_specs=[pl.BlockSpec((1, 1, tm), lambda i: (i, 0, 0), memory_space=pltpu.SMEM),
                  pl.BlockSpec(memory_space=pl.ANY)],
        out_specs=pl.BlockSpec((tm, d), lambda i: (i, 0)),
        out_shape=jax.ShapeDtypeStruct((n_tiles * tm, d), h.dtype),
        scratch_shapes=[pltpu.SemaphoreType.DMA(())],
        compiler_params=_cparams("arbitrary"), name="moe_dispatch",
    )(src, h)


def _combine_body(tt, pos_ref, y_hbm, wgt_ref, x_ref, g_ref, o_ref, buf_ref, sem):
    def issue(j, carry):
        for k in range(TOP_K):
            pltpu.make_async_copy(y_hbm.at[pl.ds(pos_ref[0, 0, TOP_K * j + k], 1), :],
                                  buf_ref.at[k, pl.ds(j, 1), :], sem).start()
        return carry

    lax.fori_loop(0, tt, issue, 0, unroll=DMA_ISSUE_UNROLL)
    for k in range(TOP_K):
        pltpu.make_async_copy(y_hbm.at[pl.ds(0, tt), :], buf_ref.at[k], sem).wait()
    w = wgt_ref[...]
    lane = lax.broadcasted_iota(I32, w.shape, 1)
    y = jnp.zeros(x_ref.shape, F32)
    for k in range(TOP_K):
        wk = jnp.sum(jnp.where(lane == k, w, 0.0), axis=-1, keepdims=True)
        y = y + wk * buf_ref[k]
    o_ref[...] = x_ref[...] + g_ref[0] * y


def _combine_residual(y_sorted, pos, wgt, x, gate, geom, first_row):
    n_ctx_rows, seq, nb = geom
    tokens = wgt.shape[0]
    d = x.shape[1]
    tt = pos.shape[2] // TOP_K
    t0 = first_row // tt
    mrow = _mod_row_fn(tt, first_row, n_ctx_rows, seq, nb)
    return pl.pallas_call(
        functools.partial(_combine_body, tt), grid=(tokens // tt,),
        in_specs=[
            pl.BlockSpec((1, 1, TOP_K * tt), lambda i: (i, 0, 0), memory_space=pltpu.SMEM),
            pl.BlockSpec(memory_space=pl.ANY),
            pl.BlockSpec((tt, LANES), lambda i: (i, 0)),
            pl.BlockSpec((tt, d), lambda i: (i + t0, 0)),
            pl.BlockSpec((1, 1, d), lambda i: (mrow(i), 0, 0)),
        ],
        out_specs=pl.BlockSpec((tt, d), lambda i: (i + t0, 0)),
        out_shape=jax.ShapeDtypeStruct(x.shape, F32),
        scratch_shapes=[pltpu.VMEM((TOP_K, tt, d), F32), pltpu.SemaphoreType.DMA(())],
        input_output_aliases={3: 0},
        compiler_params=_cparams("arbitrary"), name="moe_combine",
    )(pos, y_sorted, wgt, x, gate)


def _moe(h, idx, wgt, w1, w3, w2, x, gate, geom, first_row):
    tokens, d = h.shape
    n_exp = w1.shape[0]
    tm = _tile(512, tokens)
    tt = _tile(256, tokens, first_row if first_row else tokens, geom[0], geom[1])
    n_assign = TOP_K * tokens
    n_tiles = n_assign // tm + n_exp
    e_flat = idx[:, :TOP_K].reshape(n_assign)
    onehot = (e_flat[:, None] == jnp.arange(n_exp, dtype=I32)[None, :]).astype(I32)
    csum = jnp.cumsum(onehot, axis=0)
    rank = jnp.sum(csum * onehot, axis=1) - 1
    counts = csum[-1]
    padded = ((counts + tm - 1) // tm) * tm
    ends = jnp.cumsum(padded)
    starts = ends - padded
    pos = jnp.sum(onehot * starts[None, :], axis=1) + rank
    n_live = (ends[-1] // tm).astype(I32).reshape(1)
    tile_start = jnp.arange(n_tiles, dtype=I32) * tm
    tile_expert = jnp.sum((tile_start[:, None] >= ends[None, :]).astype(I32), axis=1)
    last_expert = jnp.max(jnp.where(counts > 0, jnp.arange(n_exp, dtype=I32), 0))
    tile_expert = jnp.minimum(tile_expert, last_expert).astype(I32)
    src = jnp.zeros((n_tiles * tm,), I32).at[pos].set(jnp.arange(n_assign, dtype=I32) // TOP_K)

    xs = _gather_rows(h, src.reshape(n_tiles, 1, tm), tm)
    ys = _ffn_grouped(xs, tile_expert, n_live, w1, w3, w2, tm)
    return _combine_residual(ys, pos.reshape(tokens // tt, 1, TOP_K * tt), wgt, x, gate, geom, first_row)


def _final_norm_body(x_ref, w_ref, o_ref):
    x = x_ref[...]
    ms = jnp.mean(x * x, axis=-1, keepdims=True)
    o_ref[...] = x * lax.rsqrt(ms + NORM_EPS) * w_ref[...]


def _final_norm(x, w, first_row):
    r, d = x.shape
    rows = r - first_row
    tm = _tile(256, rows, first_row)
    t0 = first_row // tm
    return pl.pallas_call(
        _final_norm_body, grid=(rows // tm,),
        in_specs=[pl.BlockSpec((tm, d), lambda i: (i + t0, 0)),
                  pl.BlockSpec((1, d), lambda i: (0, 0))],
        out_specs=pl.BlockSpec((tm, d), lambda i: (i, 0)),
        out_shape=jax.ShapeDtypeStruct((rows, d), F32),
        compiler_params=_cparams("parallel"), name="final_norm",
    )(x, w.reshape(1, d))


def kernel(x, c, ctx, c_ctx, w_mod, b_mod, norm1_w, norm2_w, w_in, conv_w, conv_b, conv_ln_w,
           conv_ln_b, ret_decay_logit, ret_gn_w, ret_gn_b, w_out, ffn_w1, ffn_w3, ffn_w2,
           moe_router, moe_router_b, moe_w1, moe_w3, moe_w2, final_w):
    nb, seq, d = x.shape
    n_ctx = ctx.shape[1]
    depth = w_mod.shape[0]
    cw = conv_w.shape[2]
    n_heads = ret_decay_logit.shape[2]
    dh = ret_gn_w.shape[1] // n_heads
    n_ctx_rows = nb * n_ctx
    geom = (n_ctx_rows, seq, nb)
    q_col0 = 2 * cw // LANES

    xs = jnp.concatenate([ctx.reshape(n_ctx_rows, d), x.reshape(nb * seq, d)], axis=0)

    rm = -(-(nb + 1) // SUBLANES) * SUBLANES
    c_all = jnp.zeros((rm, d), F32).at[:nb].set(c).at[nb].set(c_ctx)
    mods = _modulation(c_all, w_mod, b_mod).reshape(depth, rm, 6, 1, d)

    rope_tabs = _rope_tables(seq, dh)

    for i in range(depth):
        last = i == depth - 1
        j = i // 2
        sh1, sc1, g1, sh2, sc2, g2 = (mods[i, :, m] for m in range(6))

        h = _norm_mod(xs, norm1_w[i], sc1, sh1, geom, 0, BF16)
        u = _matmul(h, w_in[i].astype(BF16), BF16)
        w_out_b = w_out[i].astype(BF16)

        ret_ctx, ctx_state = _retention(u, 0, n_ctx, nb, q_col0, ret_decay_logit[i],
                                        ret_gn_w[i], ret_gn_b[i], None, None)
        ret_lat, _ = _retention(u, n_ctx_rows, seq, nb, q_col0, ret_decay_logit[i],
                                ret_gn_w[i], ret_gn_b[i], rope_tabs, ctx_state)
        conv_lat = _conformer_conv(u, n_ctx_rows, seq, nb, conv_w[i], conv_b[i],
                                   conv_ln_w[i], conv_ln_b[i])
        xs = _out_proj_residual(conv_lat, ret_lat, w_out_b, xs, g1, geom, n_ctx_rows)
        if not last:
            conv_ctx = _conformer_conv(u, 0, n_ctx, nb, conv_w[i], conv_b[i],
                                       conv_ln_w[i], conv_ln_b[i])
            xs = _out_proj_residual(conv_ctx, ret_ctx, w_out_b, xs, g1, geom, 0)

        first_row = n_ctx_rows if last else 0
        if i % 2 == 0:
            h2 = _norm_mod(xs, norm2_w[i], sc2, sh2, geom, first_row, BF16)
            w1, w3, w2 = (w[j:j + 1].astype(BF16) for w in (ffn_w1, ffn_w3, ffn_w2))
            xs = _ffn_dense(h2, w1, w3, w2, xs, g2, geom, first_row)
        else:
            h2, idx, wgt = _norm_mod(xs, norm2_w[i], sc2, sh2, geom, first_row, F32,
                                     router=(moe_router[j], moe_router_b[j]))
            w1, w3, w2 = (w[j].astype(BF16) for w in (moe_w1, moe_w3, moe_w2))
            xs = _moe(h2, idx, wgt, w1, w3, w2, xs, g2, geom, first_row)

    return _final_norm(xs, final_w, n_ctx_rows).reshape(nb, seq, d)
```

```python
import functools

import jax
import jax.numpy as jnp
from jax import lax
from jax.experimental import pallas as pl
from jax.experimental.pallas import tpu as pltpu

F32 = jnp.float32
BF16 = jnp.bfloat16
I32 = jnp.int32

GRID_W = 64
ROPE_BASE = 10000.0
NORM_EPS = 1e-6
TOP_K = 2
LANES = 128
SUBLANES = 8
VMEM_LIMIT_BYTES = 56 * 1024 * 1024


def _cparams(*sem):
    return pltpu.CompilerParams(dimension_semantics=sem, vmem_limit_bytes=VMEM_LIMIT_BYTES)


def _tile(pref, *dims):
    t = pref
    while any(d % t for d in dims):
        t //= 2
    return t


def _sigmoid(v):
    return 1.0 / (1.0 + jnp.exp(-v))


def _log_sigmoid(v):
    return jnp.minimum(v, 0.0) - jnp.log1p(jnp.exp(-jnp.abs(v)))


def _mod_body(c_ref, w_ref, b_ref, o_ref):
    c = c_ref[...]
    s = (c * _sigmoid(c)).astype(BF16)
    o_ref[0] = jnp.dot(s, w_ref[0].astype(BF16), preferred_element_type=F32) + b_ref[0]


def _modulation(c_all, w_mod, b_mod):
    depth, d, n6 = w_mod.shape
    rm = c_all.shape[0]
    tn = _tile(1024, n6)
    return pl.pallas_call(
        _mod_body,
        grid=(depth, n6 // tn),
        in_specs=[
            pl.BlockSpec((rm, d), lambda l, j: (0, 0)),
            pl.BlockSpec((1, d, tn), lambda l, j: (l, 0, j)),
            pl.BlockSpec((1, 1, tn), lambda l, j: (l, 0, j)),
        ],
        out_specs=pl.BlockSpec((1, rm, tn), lambda l, j: (l, 0, j)),
        out_shape=jax.ShapeDtypeStruct((depth, rm, n6), F32),
        compiler_params=_cparams("parallel", "parallel"),
        name="modulation",
    )(c_all, w_mod, b_mod.reshape(depth, 1, n6))


def _mod_row_fn(tm, first_row, n_ctx_rows, seq, nb):
    t0 = first_row // tm
    n_ctx_tiles = n_ctx_rows // tm
    per_seq = seq // tm

    def f(i):
        g = i + t0
        return jnp.where(g < n_ctx_tiles, nb, (g - n_ctx_tiles) // per_seq)

    return f


def _modulated_norm(x, nw, sc, sh):
    ms = jnp.mean(x * x, axis=-1, keepdims=True)
    y = x * lax.rsqrt(ms + NORM_EPS) * nw
    return y * (1.0 + sc) + sh


def _norm_mod_body(x_ref, nw_ref, sc_ref, sh_ref, h_ref):
    h = _modulated_norm(x_ref[...], nw_ref[...], sc_ref[0], sh_ref[0])
    h_ref[...] = h.astype(h_ref.dtype)


def _router_top2(h, wr, br, n_exp):
    logits = jnp.dot(h, wr, preferred_element_type=F32, precision=lax.Precision.HIGHEST) + br
    lane = lax.broadcasted_iota(I32, logits.shape, 1)
    neg = jnp.float32(-jnp.inf)
    logits = jnp.where(lane < n_exp, logits, neg)
    m1 = jnp.max(logits, axis=-1, keepdims=True)
    i1 = jnp.min(jnp.where(logits == m1, lane, LANES), axis=-1, keepdims=True)
    rest = jnp.where(lane == i1, neg, logits)
    m2 = jnp.max(rest, axis=-1, keepdims=True)
    i2 = jnp.min(jnp.where(rest == m2, lane, LANES), axis=-1, keepdims=True)
    e = jnp.exp(m2 - m1)
    w1 = 1.0 / (1.0 + e)
    w2 = e / (1.0 + e)
    idx = jnp.where(lane == 0, i1, jnp.where(lane == 1, i2, 0))
    wgt = jnp.where(lane == 0, w1, jnp.where(lane == 1, w2, 0.0))
    return idx, wgt


def _norm_mod_router_body(n_exp, x_ref, nw_ref, sc_ref, sh_ref, wr_ref, br_ref,
                          h_ref, idx_ref, wgt_ref):
    h = _modulated_norm(x_ref[...], nw_ref[...], sc_ref[0], sh_ref[0])
    h_ref[...] = h.astype(h_ref.dtype)
    idx, wgt = _router_top2(h, wr_ref[...], br_ref[...], n_exp)
    idx_ref[...] = idx
    wgt_ref[...] = wgt


def _norm_mod(x, nw, sc, sh, geom, first_row, out_dtype, router=None):
    n_ctx_rows, seq, nb = geom
    r, d = x.shape
    rows = r - first_row
    tm = _tile(256, rows, first_row if first_row else rows, n_ctx_rows, seq)
    t0 = first_row // tm
    mrow = _mod_row_fn(tm, first_row, n_ctx_rows, seq, nb)
    in_specs = [
        pl.BlockSpec((tm, d), lambda i: (i + t0, 0)),
        pl.BlockSpec((1, d), lambda i: (0, 0)),
        pl.BlockSpec((1, 1, d), lambda i: (mrow(i), 0, 0)),
        pl.BlockSpec((1, 1, d), lambda i: (mrow(i), 0, 0)),
    ]
    args = [x, nw.reshape(1, d), sc, sh]
    row_spec = pl.BlockSpec((tm, d), lambda i: (i, 0))
    if router is None:
        return pl.pallas_call(
            _norm_mod_body, grid=(rows // tm,), in_specs=in_specs, out_specs=row_spec,
            out_shape=jax.ShapeDtypeStruct((rows, d), out_dtype),
            compiler_params=_cparams("parallel"), name="norm_mod",
        )(*args)
    wr, br = router
    n_exp = wr.shape[1]
    wr_p = jnp.zeros((d, LANES), F32).at[:, :n_exp].set(wr)
    br_p = jnp.zeros((1, LANES), F32).at[0, :n_exp].set(br)
    in_specs += [pl.BlockSpec((d, LANES), lambda i: (0, 0)),
                 pl.BlockSpec((1, LANES), lambda i: (0, 0))]
    lane_spec = pl.BlockSpec((tm, LANES), lambda i: (i, 0))
    return pl.pallas_call(
        functools.partial(_norm_mod_router_body, n_exp),
        grid=(rows // tm,), in_specs=in_specs,
        out_specs=[row_spec, lane_spec, lane_spec],
        out_shape=[jax.ShapeDtypeStruct((rows, d), out_dtype),
                   jax.ShapeDtypeStruct((rows, LANES), I32),
                   jax.ShapeDtypeStruct((rows, LANES), F32)],
        compiler_params=_cparams("parallel"), name="norm_mod_router",
    )(*args, wr_p, br_p)


def _matmul_body(a_ref, w_ref, o_ref):
    o_ref[...] = jnp.dot(a_ref[...], w_ref[0], preferred_element_type=F32).astype(o_ref.dtype)


def _matmul(a, w, layer, out_dtype):
    m, k = a.shape
    n = w.shape[2]
    tm = _tile(1024, m)
    tn = _tile(1024, n)
    return pl.pallas_call(
        _matmul_body, grid=(m // tm, n // tn),
        in_specs=[pl.BlockSpec((tm, k), lambda i, j: (i, 0)),
                  pl.BlockSpec((1, k, tn), lambda i, j: (layer, 0, j))],
        out_specs=pl.BlockSpec((tm, tn), lambda i, j: (i, j)),
        out_shape=jax.ShapeDtypeStruct((m, n), out_dtype),
        compiler_params=_cparams("parallel", "parallel"), name="in_proj",
    )(a, w)


CONV_ROWS = 64
CONV_PAD = 16


def _conv_body(n, kw, a_ref, b_ref, cw_ref, cb_ref, lw_ref, lb_ref, o_ref, g_ref, acc_ref):
    cw = a_ref.shape[1]
    half = kw // 2
    zeros = jnp.zeros((CONV_PAD, cw), F32)
    g_ref[pl.ds(0, CONV_PAD), :] = zeros
    g_ref[pl.ds(CONV_PAD + n, CONV_PAD), :] = zeros

    def glu_step(i, carry):
        r = pl.multiple_of(i * CONV_ROWS, CONV_ROWS)
        a = a_ref[pl.ds(r, CONV_ROWS), :].astype(F32)
        b = b_ref[pl.ds(r, CONV_ROWS), :].astype(F32)
        g_ref[pl.ds(r + CONV_PAD, CONV_ROWS), :] = a * _sigmoid(b)
        return carry

    lax.fori_loop(0, n // CONV_ROWS, glu_step, 0)

    def conv_step(i, carry):
        r = pl.multiple_of(i * CONV_ROWS, CONV_ROWS)
        for c in range(cw // LANES):
            lanes = pl.ds(c * LANES, LANES)
            win = g_ref[pl.ds(r, CONV_ROWS + 2 * CONV_PAD), lanes]
            out = jnp.zeros((CONV_ROWS, LANES), F32) + cb_ref[:, lanes]
            for s in range(SUBLANES):
                part = None
                for q in range(2 * CONV_PAD // SUBLANES):
                    k = SUBLANES * q + s - (CONV_PAD - half)
                    if 0 <= k < kw:
                        term = (win[SUBLANES * q:SUBLANES * q + CONV_ROWS + SUBLANES, :]
                                * cw_ref[pl.ds(k, 1), lanes])
                        part = term if part is None else part + term
                out = out + part[s:s + CONV_ROWS, :]
            acc_ref[:, lanes] = out
        acc = acc_ref[...]
        mu = jnp.mean(acc, axis=-1, keepdims=True)
        cen = acc - mu
        var = jnp.mean(cen * cen, axis=-1, keepdims=True)
        y = cen * lax.rsqrt(var + NORM_EPS) * lw_ref[...] + lb_ref[...]
        o_ref[pl.ds(r, CONV_ROWS), :] = (y * _sigmoid(y)).astype(o_ref.dtype)
        return carry

    lax.fori_loop(0, n // CONV_ROWS, conv_step, 0)


def _conformer_conv(u, first_row, n, nb, conv_w, conv_b, ln_w, ln_b):
    kw, cw = conv_w.shape
    assert kw // 2 < CONV_PAD and n % CONV_ROWS == 0 and first_row % n == 0
    b0 = first_row // n
    return pl.pallas_call(
        functools.partial(_conv_body, n, kw),
        grid=(nb,),
        in_specs=[
            pl.BlockSpec((n, cw), lambda b: (b + b0, 0)),
            pl.BlockSpec((n, cw), lambda b: (b + b0, 1)),
            pl.BlockSpec((kw, cw), lambda b: (0, 0)),
            pl.BlockSpec((1, cw), lambda b: (0, 0)),
            pl.BlockSpec((1, cw), lambda b: (0, 0)),
            pl.BlockSpec((1, cw), lambda b: (0, 0)),
        ],
        out_specs=pl.BlockSpec((n, cw), lambda b: (b, 0)),
        out_shape=jax.ShapeDtypeStruct((nb * n, cw), BF16),
        scratch_shapes=[pltpu.VMEM((n + 2 * CONV_PAD, cw), F32), pltpu.VMEM((CONV_ROWS, cw), F32)],
        compiler_params=_cparams("parallel"), name="conformer_conv",
    )(u, u, conv_w, conv_b.reshape(1, cw), ln_w.reshape(1, cw), ln_b.reshape(1, cw))


RET_CHUNK_ROWS = 256
RET_UNROLL = 4


def _retention_body(n, chunk, rope, has_init, k_scale, *refs):
    dl_ref, q_ref, k_ref, v_ref, gate_ref, gw_ref, gb_ref = refs[:7]
    pos = 7
    if rope:
        cos_ref, sin_ref = refs[pos:pos + 2]
        pos += 2
    if has_init:
        s0_ref = refs[pos]
        pos += 1
    o_ref, sout_ref, qs_ref, acc_ref, u_ref, st_ref, decay_ref, vec_ref, cd_ref = refs[pos:pos + 9]
    dh = q_ref.shape[1]
    nc = n // chunk
    h = pl.program_id(0)
    nt = (((1,), (1,)), ((), ()))
    tn = (((0,), (0,)), ((), ()))

    def rows(c):
        return pl.ds(pl.multiple_of(c * chunk, chunk), chunk)

    @pl.when(pl.program_id(1) == 0)
    def _():
        def log_decay(direction, shape):
            return _log_sigmoid(jnp.full(shape, dl_ref[direction, h], F32))

        ii = lax.broadcasted_iota(I32, (chunk, chunk), 0)
        jj = lax.broadcasted_iota(I32, (chunk, chunk), 1)
        diff = (ii - jj).astype(F32)
        decay_ref[...] = jnp.where(
            diff >= 0.0,
            jnp.exp(jnp.maximum(diff, 0.0) * log_decay(0, (chunk, chunk))),
            jnp.exp(jnp.maximum(-diff, 0.0) * log_decay(1, (chunk, chunk))))
        idx = lax.broadcasted_iota(I32, (chunk, dh), 0).astype(F32)
        lg_f = log_decay(0, (chunk, dh))
        lg_b = log_decay(1, (chunk, dh))
        vec_ref[0] = jnp.exp((idx + 1.0) * lg_f)
        vec_ref[1] = jnp.exp((chunk - 1.0 - idx) * lg_f)
        vec_ref[2] = jnp.exp((chunk - idx) * lg_b)
        vec_ref[3] = jnp.exp(idx * lg_b)
        cd_ref[0] = jnp.exp(chunk * log_decay(0, (dh, dh)))
        cd_ref[1] = jnp.exp(chunk * log_decay(1, (dh, dh)))

    def intra(c, carry):
        rs = rows(c)
        q = q_ref[rs, :].astype(F32)
        k = k_ref[rs, :].astype(F32)
        if rope:
            lane = lax.broadcasted_iota(I32, (chunk, dh), 1)
            first = (lane % (dh // 2)) < (dh // 4)
            cos = cos_ref[rs, :]
            sin = sin_ref[rs, :]

            def rot(t):
                return jnp.where(first, -pltpu.roll(t, dh - dh // 4, 1), pltpu.roll(t, dh // 4, 1))

            q = q * cos + rot(q) * sin
            k = k * cos + rot(k) * sin
        qb = q.astype(BF16)
        qs_ref[rs, :] = qb
        kf = k * k_scale
        v = v_ref[rs, :]
        s = lax.dot_general(qb, kf.astype(BF16), nt, preferred_element_type=F32)
        acc_ref[rs, :] = jnp.dot((s * decay_ref[...]).astype(BF16), v, preferred_element_type=F32)
        kd = jnp.concatenate([kf * vec_ref[1], kf * vec_ref[3]], axis=1).astype(BF16)
        u_ref[c] = lax.dot_general(kd, v, tn, preferred_element_type=F32)
        return carry

    lax.fori_loop(0, nc, intra, 0, unroll=min(nc, RET_UNROLL))

    if has_init:
        init_f = s0_ref[0, 0, 0]
        init_b = s0_ref[0, 0, 1]
    else:
        init_f = jnp.zeros((dh, dh), F32)
        init_b = jnp.zeros((dh, dh), F32)

    def scan_f(c, state):
        st_ref[c, :, pl.ds(0, dh)] = state.astype(BF16)
        return state * cd_ref[0] + u_ref[c, pl.ds(0, dh), :]

    def scan_b(t, state):
        c = nc - 1 - t
        st_ref[c, :, pl.ds(dh, dh)] = state.astype(BF16)
        return state * cd_ref[1] + u_ref[c, pl.ds(dh, dh), :]

    sout_ref[0, 0, 0] = lax.fori_loop(0, nc, scan_f, init_f)
    sout_ref[0, 0, 1] = lax.fori_loop(0, nc, scan_b, init_b)

    def readout(c, carry):
        rs = rows(c)
        r = jnp.dot(qs_ref[rs, :], st_ref[c], preferred_element_type=F32)
        o = acc_ref[rs, :] + r[:, :dh] * vec_ref[0] + r[:, dh:] * vec_ref[2]
        mu = jnp.mean(o, axis=-1, keepdims=True)
        cen = o - mu
        var = jnp.mean(cen * cen, axis=-1, keepdims=True)
        on = cen * lax.rsqrt(var + NORM_EPS) * gw_ref[...] + gb_ref[...]
        g = gate_ref[rs, :].astype(F32)
        o_ref[rs, :] = (on * (g * _sigmoid(g))).astype(o_ref.dtype)
        return carry

    lax.fori_loop(0, nc, readout, 0, unroll=min(nc, RET_UNROLL))


def _retention(u, first_row, n, nb, col0, decay_logit, gn_w, gn_b, rope_tabs, init_state):
    n_heads = decay_logit.shape[1]
    dh = gn_w.shape[0] // n_heads
    assert dh == LANES and first_row % n == 0
    chunk = _tile(RET_CHUNK_ROWS, n)
    nc = n // chunk
    b0 = first_row // n
    rope = rope_tabs is not None
    has_init = init_state is not None

    def col(kind):
        return lambda h, b: (b + b0, col0 + kind * n_heads + h)

    in_specs = [
        pl.BlockSpec(memory_space=pltpu.SMEM),
        pl.BlockSpec((n, dh), col(0)),
        pl.BlockSpec((n, dh), col(1)),
        pl.BlockSpec((n, dh), col(2)),
        pl.BlockSpec((n, dh), col(3)),
        pl.BlockSpec((1, dh), lambda h, b: (0, h)),
        pl.BlockSpec((1, dh), lambda h, b: (0, h)),
    ]
    args = [decay_logit, u, u, u, u, gn_w.reshape(1, -1), gn_b.reshape(1, -1)]
    if rope:
        in_specs += [pl.BlockSpec((n, dh), lambda h, b: (0, 0))] * 2
        args += list(rope_tabs)
    state_spec = pl.BlockSpec((1, 1, 2, dh, dh), lambda h, b: (b, h, 0, 0, 0))
    if has_init:
        in_specs.append(state_spec)
        args.append(init_state)
    return pl.pallas_call(
        functools.partial(_retention_body, n, chunk, rope, has_init, float(dh) ** -0.5),
        grid=(n_heads, nb),
        in_specs=in_specs,
        out_specs=[pl.BlockSpec((n, dh), lambda h, b: (b, h)), state_spec],
        out_shape=[jax.ShapeDtypeStruct((nb * n, n_heads * dh), BF16),
                   jax.ShapeDtypeStruct((nb, n_heads, 2, dh, dh), F32)],
        scratch_shapes=[
            pltpu.VMEM((n, dh), BF16),
            pltpu.VMEM((n, dh), F32),
            pltpu.VMEM((nc, 2 * dh, dh), F32),
            pltpu.VMEM((nc, dh, 2 * dh), BF16),
            pltpu.VMEM((chunk, chunk), F32),
            pltpu.VMEM((4, chunk, dh), F32),
            pltpu.VMEM((2, dh, dh), F32),
        ],
        compiler_params=_cparams("arbitrary", "arbitrary"), name="retention",
    )(*args)


def _rope_tables(n, dh):
    rows = n // GRID_W
    row = jnp.repeat(jnp.arange(rows, dtype=F32), GRID_W)
    colp = jnp.tile(jnp.arange(GRID_W, dtype=F32), rows)
    n_freq = dh // 4
    inv_freq = ROPE_BASE ** (-jnp.arange(n_freq, dtype=F32) / n_freq)
    ang = jnp.stack([row[:, None] * inv_freq, colp[:, None] * inv_freq], axis=1)
    ang = jnp.broadcast_to(ang[:, :, None, :], (rows * GRID_W, 2, 2, n_freq)).reshape(rows * GRID_W, dh)
    return jnp.cos(ang), jnp.sin(ang)


def _out_proj_body(a1_ref, a2_ref, w_ref, x_ref, g_ref, o_ref):
    k1 = a1_ref.shape[1]
    y = jnp.dot(a1_ref[...], w_ref[0, pl.ds(0, k1), :], preferred_element_type=F32)
    y = y + jnp.dot(a2_ref[...], w_ref[0, pl.ds(k1, a2_ref.shape[1]), :], preferred_element_type=F32)
    o_ref[...] = x_ref[...] + g_ref[0] * y


def _out_proj_residual(a1, a2, w, layer, x, gate, geom, first_row):
    n_ctx_rows, seq, nb = geom
    rows, k1 = a1.shape
    k2 = a2.shape[1]
    d = w.shape[2]
    tm = _tile(512, rows, first_row if first_row else rows, n_ctx_rows, seq)
    t0 = first_row // tm
    mrow = _mod_row_fn(tm, first_row, n_ctx_rows, seq, nb)
    return pl.pallas_call(
        _out_proj_body, grid=(rows // tm,),
        in_specs=[
            pl.BlockSpec((tm, k1), lambda i: (i, 0)),
            pl.BlockSpec((tm, k2), lambda i: (i, 0)),
            pl.BlockSpec((1, k1 + k2, d), lambda i: (layer, 0, 0)),
            pl.BlockSpec((tm, d), lambda i: (i + t0, 0)),
            pl.BlockSpec((1, 1, d), lambda i: (mrow(i), 0, 0)),
        ],
        out_specs=pl.BlockSpec((tm, d), lambda i: (i + t0, 0)),
        out_shape=jax.ShapeDtypeStruct(x.shape, F32),
        input_output_aliases={3: 0},
        compiler_params=_cparams("parallel"), name="out_proj",
    )(a1, a2, w, x, gate)


def _swiglu_partial(xb, w1, w3, w2):
    h1 = jnp.dot(xb, w1, preferred_element_type=F32)
    h3 = jnp.dot(xb, w3, preferred_element_type=F32)
    t = (h1 * _sigmoid(h1) * h3).astype(BF16)
    return jnp.dot(t, w2, preferred_element_type=F32)


def _ffn_dense_body(h_ref, w1_ref, w3_ref, w2_ref, x_ref, g_ref, o_ref, acc_ref):
    f = pl.program_id(1)

    @pl.when(f == 0)
    def _():
        acc_ref[...] = jnp.zeros_like(acc_ref)

    acc_ref[...] += _swiglu_partial(h_ref[...], w1_ref[0], w3_ref[0], w2_ref[0])

    @pl.when(f == pl.num_programs(1) - 1)
    def _():
        o_ref[...] = x_ref[...] + g_ref[0] * acc_ref[...]


def _ffn_dense(h, w1, w3, w2, layer, x, gate, geom, first_row):
    n_ctx_rows, seq, nb = geom
    rows, d = h.shape
    fdim = w1.shape[2]
    tm = _tile(512, rows, first_row if first_row else rows, n_ctx_rows, seq)
    tf = _tile(512, fdim)
    t0 = first_row // tm
    mrow = _mod_row_fn(tm, first_row, n_ctx_rows, seq, nb)
    return pl.pallas_call(
        _ffn_dense_body, grid=(rows // tm, fdim // tf),
        in_specs=[
            pl.BlockSpec((tm, d), lambda i, f: (i, 0)),
            pl.BlockSpec((1, d, tf), lambda i, f: (layer, 0, f)),
            pl.BlockSpec((1, d, tf), lambda i, f: (layer, 0, f)),
            pl.BlockSpec((1, tf, d), lambda i, f: (layer, f, 0)),
            pl.BlockSpec((tm, d), lambda i, f: (i + t0, 0)),
            pl.BlockSpec((1, 1, d), lambda i, f: (mrow(i), 0, 0)),
        ],
        out_specs=pl.BlockSpec((tm, d), lambda i, f: (i + t0, 0)),
        out_shape=jax.ShapeDtypeStruct(x.shape, F32),
        scratch_shapes=[pltpu.VMEM((tm, d), F32)],
        input_output_aliases={4: 0},
        compiler_params=_cparams("parallel", "arbitrary"), name="ffn_dense",
    )(h, w1, w3, w2, x, gate)


DMA_ISSUE_UNROLL = 8


def _ffn_grouped_body(n_tiles, tm, te_ref, nv_ref, src0_ref, srcn_ref, dstp_ref, dstl_ref,
                      h_hbm, w1_ref, w3_ref, w2_ref, yk_hbm,
                      xg_ref, xb_ref, acc_ref, stage_ref, gsem, ssem):
    i = pl.program_id(0)
    f = pl.program_id(1)
    nf = pl.num_programs(1)
    per = tm // nf
    live = i < nv_ref[0]
    slot = i % 2
    nslot = 1 - slot

    def gather_row(src_ref, r, to_slot):
        return pltpu.make_async_copy(h_hbm.at[pl.ds(src_ref[0, 0, r], 1), :],
                                     xg_ref.at[to_slot, pl.ds(r, 1), :], gsem.at[to_slot])

    def scatter_row(dst_ref, r):
        return pltpu.make_async_copy(stage_ref.at[pl.ds(r, 1), :],
                                     yk_hbm.at[pl.ds(dst_ref[0, 0, r], 1), :], ssem)

    def gather_tile_wait(at_slot):
        pltpu.make_async_copy(h_hbm.at[pl.ds(0, tm), :], xg_ref.at[at_slot], gsem.at[at_slot]).wait()

    def scatter_tile_wait():
        pltpu.make_async_copy(stage_ref, yk_hbm.at[pl.ds(0, tm), :], ssem).wait()

    @pl.when((i == 0) & (f == 0))
    def _():
        def issue(j, carry):
            gather_row(src0_ref, j, 0).start()
            return carry

        lax.fori_loop(0, tm, issue, 0, unroll=DMA_ISSUE_UNROLL)
        stage_ref[...] = jnp.zeros_like(stage_ref)

    @pl.when(f == 0)
    def _():
        gather_tile_wait(slot)
        xb_ref[...] = xg_ref[slot].astype(BF16)
        acc_ref[...] = jnp.zeros_like(acc_ref)

    def issue_step_copies():
        base = f * per
        for j in range(per):
            gather_row(srcn_ref, base + j, nslot).start()
            scatter_row(dstp_ref, base + j).start()

    @pl.when(live)
    def _():
        issue_step_copies()
        acc_ref[...] += _swiglu_partial(xb_ref[...], w1_ref[0, 0], w3_ref[0, 0], w2_ref[0, 0])

    @pl.when(jnp.logical_not(live))
    def _():
        issue_step_copies()

    @pl.when(f == nf - 1)
    def _():
        scatter_tile_wait()
        stage_ref[...] = acc_ref[...]

    @pl.when((f == nf - 1) & (i == n_tiles - 1))
    def _():
        def issue(j, carry):
            scatter_row(dstl_ref, j).start()
            return carry

        lax.fori_loop(0, tm, issue, 0, unroll=DMA_ISSUE_UNROLL)
        scatter_tile_wait()
        gather_tile_wait(nslot)


def _ffn_grouped(h, src, dst, tile_expert, n_live, w1, w3, w2, layer, tm, yk_rows):
    n_tiles = src.shape[0]
    d = h.shape[1]
    fdim = w1.shape[3]
    tf = _tile(512, fdim)
    nf = fdim // tf
    assert tm % nf == 0

    def wcol(i, f, te, nv):
        return jnp.where(i < nv[0], f, nf - 1)

    def smem_tile(index_map):
        return pl.BlockSpec((1, 1, tm), index_map, memory_space=pltpu.SMEM)

    grid_spec = pltpu.PrefetchScalarGridSpec(
        num_scalar_prefetch=2, grid=(n_tiles, nf),
        in_specs=[
            smem_tile(lambda i, f, te, nv: (0, 0, 0)),
            smem_tile(lambda i, f, te, nv: (jnp.minimum(i + 1, n_tiles - 1), 0, 0)),
            smem_tile(lambda i, f, te, nv: (i, 0, 0)),
            smem_tile(lambda i, f, te, nv: (n_tiles, 0, 0)),
            pl.BlockSpec(memory_space=pl.ANY),
            pl.BlockSpec((1, 1, d, tf), lambda i, f, te, nv: (layer, te[i], 0, wcol(i, f, te, nv))),
            pl.BlockSpec((1, 1, d, tf), lambda i, f, te, nv: (layer, te[i], 0, wcol(i, f, te, nv))),
            pl.BlockSpec((1, 1, tf, d), lambda i, f, te, nv: (layer, te[i], wcol(i, f, te, nv), 0)),
        ],
        out_specs=pl.BlockSpec(memory_space=pl.ANY),
        scratch_shapes=[
            pltpu.VMEM((2, tm, d), F32),
            pltpu.VMEM((tm, d), BF16),
            pltpu.VMEM((tm, d), F32),
            pltpu.VMEM((tm, d), F32),
            pltpu.SemaphoreType.DMA((2,)),
            pltpu.SemaphoreType.DMA(()),
        ],
    )
    return pl.pallas_call(
        functools.partial(_ffn_grouped_body, n_tiles, tm), grid_spec=grid_spec,
        out_shape=jax.ShapeDtypeStruct((yk_rows, d), F32),
        compiler_params=_cparams("arbitrary", "arbitrary"), name="ffn_grouped",
    )(tile_expert, n_live, src, src, dst, dst, h, w1, w3, w2)


def _combine_body(*refs):
    y_refs = refs[:TOP_K]
    wgt_ref, x_ref, g_ref, o_ref = refs[TOP_K:]
    w = wgt_ref[...]
    lane = lax.broadcasted_iota(I32, w.shape, 1)
    y = jnp.zeros(x_ref.shape, F32)
    for k in range(TOP_K):
        wk = jnp.sum(jnp.where(lane == k, w, 0.0), axis=-1, keepdims=True)
        y = y + wk * y_refs[k][...]
    o_ref[...] = x_ref[...] + g_ref[0] * y


def _combine_residual(yk, wgt, x, gate, geom, first_row):
    n_ctx_rows, seq, nb = geom
    tokens = wgt.shape[0]
    d = x.shape[1]
    tt = _tile(256, tokens, first_row if first_row else tokens, n_ctx_rows, seq)
    t0 = first_row // tt
    mrow = _mod_row_fn(tt, first_row, n_ctx_rows, seq, nb)
    per_slot = tokens // tt
    slot_specs = [pl.BlockSpec((tt, d), functools.partial(lambda k, i: (i + k * per_slot, 0), k))
                  for k in range(TOP_K)]
    return pl.pallas_call(
        _combine_body, grid=(per_slot,),
        in_specs=slot_specs + [
            pl.BlockSpec((tt, LANES), lambda i: (i, 0)),
            pl.BlockSpec((tt, d), lambda i: (i + t0, 0)),
            pl.BlockSpec((1, 1, d), lambda i: (mrow(i), 0, 0)),
        ],
        out_specs=pl.BlockSpec((tt, d), lambda i: (i + t0, 0)),
        out_shape=jax.ShapeDtypeStruct(x.shape, F32),
        input_output_aliases={TOP_K + 1: 0},
        compiler_params=_cparams("parallel"), name="moe_combine",
    )(*([yk] * TOP_K), wgt, x, gate)


def _moe(h, idx, wgt, w1, w3, w2, layer, x, gate, geom, first_row):
    tokens, d = h.shape
    n_exp = w1.shape[1]
    tm = _tile(512, tokens)
    n_assign = TOP_K * tokens
    n_tiles = n_assign // tm + n_exp
    e_flat = idx[:, :TOP_K].reshape(n_assign)
    onehot = (e_flat[:, None] == jnp.arange(n_exp, dtype=I32)[None, :]).astype(I32)
    csum = jnp.cumsum(onehot, axis=0)
    rank = jnp.sum(csum * onehot, axis=1) - 1
    counts = csum[-1]
    padded = ((counts + tm - 1) // tm) * tm
    ends = jnp.cumsum(padded)
    starts = ends - padded
    pos = jnp.sum(onehot * starts[None, :], axis=1) + rank
    n_live = (ends[-1] // tm).astype(I32).reshape(1)
    tile_start = jnp.arange(n_tiles, dtype=I32) * tm
    tile_expert = jnp.sum((tile_start[:, None] >= ends[None, :]).astype(I32), axis=1)
    last_expert = jnp.max(jnp.where(counts > 0, jnp.arange(n_exp, dtype=I32), 0))
    tile_expert = jnp.minimum(tile_expert, last_expert).astype(I32)
    n_rows = n_tiles * tm
    assign_of_row = jnp.full((n_rows,), n_assign, I32).at[pos].set(jnp.arange(n_assign, dtype=I32))
    is_pad = assign_of_row >= n_assign
    src = jnp.where(is_pad, 0, assign_of_row // TOP_K)
    pad_ordinal = jnp.cumsum(is_pad.astype(I32)) - 1
    yk_row = jnp.where(is_pad, n_assign + pad_ordinal,
                       (assign_of_row % TOP_K) * tokens + assign_of_row // TOP_K)
    spare = n_rows + jnp.arange(tm, dtype=I32)
    dst = jnp.concatenate([spare, yk_row])

    yk = _ffn_grouped(h, src.reshape(n_tiles, 1, tm), dst.reshape(n_tiles + 1, 1, tm), tile_expert, n_live,
                      w1, w3, w2, layer, tm, n_rows + tm)
    return _combine_residual(yk, wgt, x, gate, geom, first_row)


def _final_norm_body(x_ref, w_ref, o_ref):
    x = x_ref[...]
    ms = jnp.mean(x * x, axis=-1, keepdims=True)
    o_ref[...] = x * lax.rsqrt(ms + NORM_EPS) * w_ref[...]


def _final_norm(x, w, first_row):
    r, d = x.shape
    rows = r - first_row
    tm = _tile(256, rows, first_row)
    t0 = first_row // tm
    return pl.pallas_call(
        _final_norm_body, grid=(rows // tm,),
        in_specs=[pl.BlockSpec((tm, d), lambda i: (i + t0, 0)),
                  pl.BlockSpec((1, d), lambda i: (0, 0))],
        out_specs=pl.BlockSpec((tm, d), lambda i: (i, 0)),
        out_shape=jax.ShapeDtypeStruct((rows, d), F32),
        compiler_params=_cparams("parallel"), name="final_norm",
    )(x, w.reshape(1, d))


def kernel(x, c, ctx, c_ctx, w_mod, b_mod, norm1_w, norm2_w, w_in, conv_w, conv_b, conv_ln_w,
           conv_ln_b, ret_decay_logit, ret_gn_w, ret_gn_b, w_out, ffn_w1, ffn_w3, ffn_w2,
           moe_router, moe_router_b, moe_w1, moe_w3, moe_w2, final_w):
    nb, seq, d = x.shape
    n_ctx = ctx.shape[1]
    depth = w_mod.shape[0]
    cw = conv_w.shape[2]
    n_heads = ret_decay_logit.shape[2]
    dh = ret_gn_w.shape[1] // n_heads
    n_ctx_rows = nb * n_ctx
    geom = (n_ctx_rows, seq, nb)
    q_col0 = 2 * cw // LANES

    xs = jnp.concatenate([ctx.reshape(n_ctx_rows, d), x.reshape(nb * seq, d)], axis=0)

    rm = -(-(nb + 1) // SUBLANES) * SUBLANES
    c_all = jnp.zeros((rm, d), F32).at[:nb].set(c).at[nb].set(c_ctx)
    mods = _modulation(c_all, w_mod, b_mod).reshape(depth, rm, 6, 1, d)

    rope_tabs = _rope_tables(seq, dh)
    w_in_b = w_in.astype(BF16)
    w_out_b = w_out.astype(BF16)
    ffn_b = tuple(w.astype(BF16) for w in (ffn_w1, ffn_w3, ffn_w2))
    moe_b = tuple(w.astype(BF16) for w in (moe_w1, moe_w3, moe_w2))

    for i in range(depth):
        last = i == depth - 1
        j = i // 2
        sh1, sc1, g1, sh2, sc2, g2 = (mods[i, :, m] for m in range(6))

        h = _norm_mod(xs, norm1_w[i], sc1, sh1, geom, 0, BF16)
        u = _matmul(h, w_in_b, i, BF16)

        ret_ctx, ctx_state = _retention(u, 0, n_ctx, nb, q_col0, ret_decay_logit[i],
                                        ret_gn_w[i], ret_gn_b[i], None, None)
        ret_lat, _ = _retention(u, n_ctx_rows, seq, nb, q_col0, ret_decay_logit[i],
                                ret_gn_w[i], ret_gn_b[i], rope_tabs, ctx_state)
        conv_lat = _conformer_conv(u, n_ctx_rows, seq, nb, conv_w[i], conv_b[i],
                                   conv_ln_w[i], conv_ln_b[i])
        xs = _out_proj_residual(conv_lat, ret_lat, w_out_b, i, xs, g1, geom, n_ctx_rows)
        if not last:
            conv_ctx = _conformer_conv(u, 0, n_ctx, nb, conv_w[i], conv_b[i],
                                       conv_ln_w[i], conv_ln_b[i])
            xs = _out_proj_residual(conv_ctx, ret_ctx, w_out_b, i, xs, g1, geom, 0)

        first_row = n_ctx_rows if last else 0
        if i % 2 == 0:
            h2 = _norm_mod(xs, norm2_w[i], sc2, sh2, geom, first_row, BF16)
            xs = _ffn_dense(h2, *ffn_b, j, xs, g2, geom, first_row)
        else:
            h2, idx, wgt = _norm_mod(xs, norm2_w[i], sc2, sh2, geom, first_row, F32,
                                     router=(moe_router[j], moe_router_b[j]))
            xs = _moe(h2, idx, wgt, *moe_b, j, xs, g2, geom, first_row)

    return _final_norm(xs, final_w, n_ctx_rows).reshape(nb, seq, d)
```

```python
import functools

import jax
import jax.numpy as jnp
from jax import lax
from jax.experimental import pallas as pl
from jax.experimental.pallas import tpu as pltpu

F32 = jnp.float32
BF16 = jnp.bfloat16
I32 = jnp.int32

GRID_W = 64
ROPE_BASE = 10000.0
NORM_EPS = 1e-6
TOP_K = 2
LANES = 128
SUBLANES = 8
VMEM_LIMIT_BYTES = 56 * 1024 * 1024


def _cparams(*sem):
    return pltpu.CompilerParams(dimension_semantics=sem, vmem_limit_bytes=VMEM_LIMIT_BYTES)


def _tile(pref, *dims):
    t = pref
    while any(d % t for d in dims):
        t //= 2
    return t


def _sigmoid(v):
    return 1.0 / (1.0 + jnp.exp(-v))


def _log_sigmoid(v):
    return jnp.minimum(v, 0.0) - jnp.log1p(jnp.exp(-jnp.abs(v)))


def _mod_body(c_ref, w_ref, b_ref, o_ref):
    c = c_ref[...]
    s = (c * _sigmoid(c)).astype(BF16)
    o_ref[0] = jnp.dot(s, w_ref[0].astype(BF16), preferred_element_type=F32) + b_ref[0]


def _modulation(c_all, w_mod, b_mod):
    depth, d, n6 = w_mod.shape
    rm = c_all.shape[0]
    tn = _tile(1024, n6)
    return pl.pallas_call(
        _mod_body,
        grid=(depth, n6 // tn),
        in_specs=[
            pl.BlockSpec((rm, d), lambda l, j: (0, 0)),
            pl.BlockSpec((1, d, tn), lambda l, j: (l, 0, j)),
            pl.BlockSpec((1, 1, tn), lambda l, j: (l, 0, j)),
        ],
        out_specs=pl.BlockSpec((1, rm, tn), lambda l, j: (l, 0, j)),
        out_shape=jax.ShapeDtypeStruct((depth, rm, n6), F32),
        compiler_params=_cparams("parallel", "parallel"),
        name="modulation",
    )(c_all, w_mod, b_mod.reshape(depth, 1, n6))


def _mod_row_fn(tm, first_row, n_ctx_rows, seq, nb):
    t0 = first_row // tm
    n_ctx_tiles = n_ctx_rows // tm
    per_seq = seq // tm

    def f(i):
        g = i + t0
        return jnp.where(g < n_ctx_tiles, nb, (g - n_ctx_tiles) // per_seq)

    return f


def _modulated_norm(x, nw, sc, sh):
    ms = jnp.mean(x * x, axis=-1, keepdims=True)
    y = x * lax.rsqrt(ms + NORM_EPS) * nw
    return y * (1.0 + sc) + sh


def _norm_mod_body(x_ref, nw_ref, sc_ref, sh_ref, h_ref):
    h = _modulated_norm(x_ref[...], nw_ref[...], sc_ref[0], sh_ref[0])
    h_ref[...] = h.astype(h_ref.dtype)


def _router_top2(h, wr, br, n_exp):
    logits = jnp.dot(h, wr, preferred_element_type=F32, precision=lax.Precision.HIGHEST) + br
    lane = lax.broadcasted_iota(I32, logits.shape, 1)
    neg = jnp.float32(-jnp.inf)
    logits = jnp.where(lane < n_exp, logits, neg)
    m1 = jnp.max(logits, axis=-1, keepdims=True)
    i1 = jnp.min(jnp.where(logits == m1, lane, LANES), axis=-1, keepdims=True)
    rest = jnp.where(lane == i1, neg, logits)
    m2 = jnp.max(rest, axis=-1, keepdims=True)
    i2 = jnp.min(jnp.where(rest == m2, lane, LANES), axis=-1, keepdims=True)
    e = jnp.exp(m2 - m1)
    w1 = 1.0 / (1.0 + e)
    w2 = e / (1.0 + e)
    idx = jnp.where(lane == 0, i1, jnp.where(lane == 1, i2, 0))
    wgt = jnp.where(lane == 0, w1, jnp.where(lane == 1, w2, 0.0))
    return idx, wgt


def _norm_mod_router_body(n_exp, x_ref, nw_ref, sc_ref, sh_ref, wr_ref, br_ref,
                          h_ref, idx_ref, wgt_ref):
    h = _modulated_norm(x_ref[...], nw_ref[...], sc_ref[0], sh_ref[0])
    h_ref[...] = h.astype(h_ref.dtype)
    idx, wgt = _router_top2(h, wr_ref[...], br_ref[...], n_exp)
    idx_ref[...] = idx
    wgt_ref[...] = wgt


def _norm_mod(x, nw, sc, sh, geom, first_row, out_dtype, router=None):
    n_ctx_rows, seq, nb = geom
    r, d = x.shape
    rows = r - first_row
    tm = _tile(256, rows, first_row if first_row else rows, n_ctx_rows, seq)
    t0 = first_row // tm
    mrow = _mod_row_fn(tm, first_row, n_ctx_rows, seq, nb)
    in_specs = [
        pl.BlockSpec((tm, d), lambda i: (i + t0, 0)),
        pl.BlockSpec((1, d), lambda i: (0, 0)),
        pl.BlockSpec((1, 1, d), lambda i: (mrow(i), 0, 0)),
        pl.BlockSpec((1, 1, d), lambda i: (mrow(i), 0, 0)),
    ]
    args = [x, nw.reshape(1, d), sc, sh]
    row_spec = pl.BlockSpec((tm, d), lambda i: (i, 0))
    if router is None:
        return pl.pallas_call(
            _norm_mod_body, grid=(rows // tm,), in_specs=in_specs, out_specs=row_spec,
            out_shape=jax.ShapeDtypeStruct((rows, d), out_dtype),
            compiler_params=_cparams("parallel"), name="norm_mod",
        )(*args)
    wr, br = router
    n_exp = wr.shape[1]
    wr_p = jnp.zeros((d, LANES), F32).at[:, :n_exp].set(wr)
    br_p = jnp.zeros((1, LANES), F32).at[0, :n_exp].set(br)
    in_specs += [pl.BlockSpec((d, LANES), lambda i: (0, 0)),
                 pl.BlockSpec((1, LANES), lambda i: (0, 0))]
    lane_spec = pl.BlockSpec((tm, LANES), lambda i: (i, 0))
    return pl.pallas_call(
        functools.partial(_norm_mod_router_body, n_exp),
        grid=(rows // tm,), in_specs=in_specs,
        out_specs=[row_spec, lane_spec, lane_spec],
        out_shape=[jax.ShapeDtypeStruct((rows, d), out_dtype),
                   jax.ShapeDtypeStruct((rows, LANES), I32),
                   jax.ShapeDtypeStruct((rows, LANES), F32)],
        compiler_params=_cparams("parallel"), name="norm_mod_router",
    )(*args, wr_p, br_p)


def _matmul_body(a_ref, w_ref, o_ref):
    o_ref[...] = jnp.dot(a_ref[...], w_ref[0], preferred_element_type=F32).astype(o_ref.dtype)


def _matmul(a, w, layer, out_dtype):
    m, k = a.shape
    n = w.shape[2]
    tm = _tile(1024, m)
    tn = _tile(1024, n)
    return pl.pallas_call(
        _matmul_body, grid=(m // tm, n // tn),
        in_specs=[pl.BlockSpec((tm, k), lambda i, j: (i, 0)),
                  pl.BlockSpec((1, k, tn), lambda i, j: (layer, 0, j))],
        out_specs=pl.BlockSpec((tm, tn), lambda i, j: (i, j)),
        out_shape=jax.ShapeDtypeStruct((m, n), out_dtype),
        compiler_params=_cparams("parallel", "parallel"), name="in_proj",
    )(a, w)


CONV_ROWS = 64
CONV_PAD = 16


def _conv_body(n, kw, a_ref, b_ref, cw_ref, cb_ref, lw_ref, lb_ref, o_ref, g_ref, acc_ref):
    cw = a_ref.shape[1]
    half = kw // 2
    zeros = jnp.zeros((CONV_PAD, cw), F32)
    g_ref[pl.ds(0, CONV_PAD), :] = zeros
    g_ref[pl.ds(CONV_PAD + n, CONV_PAD), :] = zeros

    def glu_step(i, carry):
        r = pl.multiple_of(i * CONV_ROWS, CONV_ROWS)
        a = a_ref[pl.ds(r, CONV_ROWS), :].astype(F32)
        b = b_ref[pl.ds(r, CONV_ROWS), :].astype(F32)
        g_ref[pl.ds(r + CONV_PAD, CONV_ROWS), :] = a * _sigmoid(b)
        return carry

    lax.fori_loop(0, n // CONV_ROWS, glu_step, 0)

    def conv_step(i, carry):
        r = pl.multiple_of(i * CONV_ROWS, CONV_ROWS)
        for c in range(cw // LANES):
            lanes = pl.ds(c * LANES, LANES)
            win = g_ref[pl.ds(r, CONV_ROWS + 2 * CONV_PAD), lanes]
            out = jnp.zeros((CONV_ROWS, LANES), F32) + cb_ref[:, lanes]
            for s in range(SUBLANES):
                part = None
                for q in range(2 * CONV_PAD // SUBLANES):
                    k = SUBLANES * q + s - (CONV_PAD - half)
                    if 0 <= k < kw:
                        term = (win[SUBLANES * q:SUBLANES * q + CONV_ROWS + SUBLANES, :]
                                * cw_ref[pl.ds(k, 1), lanes])
                        part = term if part is None else part + term
                out = out + part[s:s + CONV_ROWS, :]
            acc_ref[:, lanes] = out
        acc = acc_ref[...]
        mu = jnp.mean(acc, axis=-1, keepdims=True)
        cen = acc - mu
        var = jnp.mean(cen * cen, axis=-1, keepdims=True)
        y = cen * lax.rsqrt(var + NORM_EPS) * lw_ref[...] + lb_ref[...]
        o_ref[pl.ds(r, CONV_ROWS), :] = (y * _sigmoid(y)).astype(o_ref.dtype)
        return carry

    lax.fori_loop(0, n // CONV_ROWS, conv_step, 0)


def _conformer_conv(u, first_row, n, nb, conv_w, conv_b, ln_w, ln_b):
    kw, cw = conv_w.shape
    assert kw // 2 < CONV_PAD and n % CONV_ROWS == 0 and first_row % n == 0
    b0 = first_row // n
    return pl.pallas_call(
        functools.partial(_conv_body, n, kw),
        grid=(nb,),
        in_specs=[
            pl.BlockSpec((n, cw), lambda b: (b + b0, 0)),
            pl.BlockSpec((n, cw), lambda b: (b + b0, 1)),
            pl.BlockSpec((kw, cw), lambda b: (0, 0)),
            pl.BlockSpec((1, cw), lambda b: (0, 0)),
            pl.BlockSpec((1, cw), lambda b: (0, 0)),
            pl.BlockSpec((1, cw), lambda b: (0, 0)),
        ],
        out_specs=pl.BlockSpec((n, cw), lambda b: (b, 0)),
        out_shape=jax.ShapeDtypeStruct((nb * n, cw), BF16),
        scratch_shapes=[pltpu.VMEM((n + 2 * CONV_PAD, cw), F32), pltpu.VMEM((CONV_ROWS, cw), F32)],
        compiler_params=_cparams("parallel"), name="conformer_conv",
    )(u, u, conv_w, conv_b.reshape(1, cw), ln_w.reshape(1, cw), ln_b.reshape(1, cw))


RET_CHUNK_ROWS = 256
RET_UNROLL = 4


def _retention_body(n, chunk, rope, has_init, k_scale, *refs):
    dl_ref, q_ref, k_ref, v_ref, gate_ref, gw_ref, gb_ref = refs[:7]
    pos = 7
    if rope:
        cos_ref, sin_ref = refs[pos:pos + 2]
        pos += 2
    if has_init:
        s0_ref = refs[pos]
        pos += 1
    o_ref, sout_ref, qs_ref, acc_ref, u_ref, st_ref, decay_ref, vec_ref, cd_ref = refs[pos:pos + 9]
    dh = q_ref.shape[1]
    nc = n // chunk
    h = pl.program_id(0)
    nt = (((1,), (1,)), ((), ()))
    tn = (((0,), (0,)), ((), ()))

    def rows(c):
        return pl.ds(pl.multiple_of(c * chunk, chunk), chunk)

    @pl.when(pl.program_id(1) == 0)
    def _():
        def log_decay(direction, shape):
            return _log_sigmoid(jnp.full(shape, dl_ref[direction, h], F32))

        ii = lax.broadcasted_iota(I32, (chunk, chunk), 0)
        jj = lax.broadcasted_iota(I32, (chunk, chunk), 1)
        diff = (ii - jj).astype(F32)
        decay_ref[...] = jnp.where(
            diff >= 0.0,
            jnp.exp(jnp.maximum(diff, 0.0) * log_decay(0, (chunk, chunk))),
            jnp.exp(jnp.maximum(-diff, 0.0) * log_decay(1, (chunk, chunk))))
        idx = lax.broadcasted_iota(I32, (chunk, dh), 0).astype(F32)
        lg_f = log_decay(0, (chunk, dh))
        lg_b = log_decay(1, (chunk, dh))
        vec_ref[0] = jnp.exp((idx + 1.0) * lg_f)
        vec_ref[1] = jnp.exp((chunk - 1.0 - idx) * lg_f)
        vec_ref[2] = jnp.exp((chunk - idx) * lg_b)
        vec_ref[3] = jnp.exp(idx * lg_b)
        cd_ref[0] = jnp.exp(chunk * log_decay(0, (dh, dh)))
        cd_ref[1] = jnp.exp(chunk * log_decay(1, (dh, dh)))

    def intra(c, carry):
        rs = rows(c)
        q = q_ref[rs, :].astype(F32)
        k = k_ref[rs, :].astype(F32)
        if rope:
            lane = lax.broadcasted_iota(I32, (chunk, dh), 1)
            first = (lane % (dh // 2)) < (dh // 4)
            cos = cos_ref[rs, :]
            sin = sin_ref[rs, :]

            def rot(t):
                return jnp.where(first, -pltpu.roll(t, dh - dh // 4, 1), pltpu.roll(t, dh // 4, 1))

            q = q * cos + rot(q) * sin
            k = k * cos + rot(k) * sin
        qb = q.astype(BF16)
        qs_ref[rs, :] = qb
        kf = k * k_scale
        v = v_ref[rs, :]
        s = lax.dot_general(qb, kf.astype(BF16), nt, preferred_element_type=F32)
        acc_ref[rs, :] = jnp.dot((s * decay_ref[...]).astype(BF16), v, preferred_element_type=F32)
        kd = jnp.concatenate([kf * vec_ref[1], kf * vec_ref[3]], axis=1).astype(BF16)
        u_ref[c] = lax.dot_general(kd, v, tn, preferred_element_type=F32)
        return carry

    lax.fori_loop(0, nc, intra, 0, unroll=min(nc, RET_UNROLL))

    if has_init:
        init_f = s0_ref[0, 0, 0]
        init_b = s0_ref[0, 0, 1]
    else:
        init_f = jnp.zeros((dh, dh), F32)
        init_b = jnp.zeros((dh, dh), F32)

    def scan_f(c, state):
        st_ref[c, :, pl.ds(0, dh)] = state.astype(BF16)
        return state * cd_ref[0] + u_ref[c, pl.ds(0, dh), :]

    def scan_b(t, state):
        c = nc - 1 - t
        st_ref[c, :, pl.ds(dh, dh)] = state.astype(BF16)
        return state * cd_ref[1] + u_ref[c, pl.ds(dh, dh), :]

    sout_ref[0, 0, 0] = lax.fori_loop(0, nc, scan_f, init_f)
    sout_ref[0, 0, 1] = lax.fori_loop(0, nc, scan_b, init_b)

    def readout(c, carry):
        rs = rows(c)
        r = jnp.dot(qs_ref[rs, :], st_ref[c], preferred_element_type=F32)
        o = acc_ref[rs, :] + r[:, :dh] * vec_ref[0] + r[:, dh:] * vec_ref[2]
        mu = jnp.mean(o, axis=-1, keepdims=True)
        cen = o - mu
        var = jnp.mean(cen * cen, axis=-1, keepdims=True)
        on = cen * lax.rsqrt(var + NORM_EPS) * gw_ref[...] + gb_ref[...]
        g = gate_ref[rs, :].astype(F32)
        o_ref[rs, :] = (on * (g * _sigmoid(g))).astype(o_ref.dtype)
        return carry

    lax.fori_loop(0, nc, readout, 0, unroll=min(nc, RET_UNROLL))


def _retention(u, first_row, n, nb, col0, decay_logit, gn_w, gn_b, rope_tabs, init_state):
    n_heads = decay_logit.shape[1]
    dh = gn_w.shape[0] // n_heads
    assert dh == LANES and first_row % n == 0
    chunk = _tile(RET_CHUNK_ROWS, n)
    nc = n // chunk
    b0 = first_row // n
    rope = rope_tabs is not None
    has_init = init_state is not None

    def col(kind):
        return lambda h, b: (b + b0, col0 + kind * n_heads + h)

    in_specs = [
        pl.BlockSpec(memory_space=pltpu.SMEM),
        pl.BlockSpec((n, dh), col(0)),
        pl.BlockSpec((n, dh), col(1)),
        pl.BlockSpec((n, dh), col(2)),
        pl.BlockSpec((n, dh), col(3)),
        pl.BlockSpec((1, dh), lambda h, b: (0, h)),
        pl.BlockSpec((1, dh), lambda h, b: (0, h)),
    ]
    args = [decay_logit, u, u, u, u, gn_w.reshape(1, -1), gn_b.reshape(1, -1)]
    if rope:
        in_specs += [pl.BlockSpec((n, dh), lambda h, b: (0, 0))] * 2
        args += list(rope_tabs)
    state_spec = pl.BlockSpec((1, 1, 2, dh, dh), lambda h, b: (b, h, 0, 0, 0))
    if has_init:
        in_specs.append(state_spec)
        args.append(init_state)
    return pl.pallas_call(
        functools.partial(_retention_body, n, chunk, rope, has_init, float(dh) ** -0.5),
        grid=(n_heads, nb),
        in_specs=in_specs,
        out_specs=[pl.BlockSpec((n, dh), lambda h, b: (b, h)), state_spec],
        out_shape=[jax.ShapeDtypeStruct((nb * n, n_heads * dh), BF16),
                   jax.ShapeDtypeStruct((nb, n_heads, 2, dh, dh), F32)],
        scratch_shapes=[
            pltpu.VMEM((n, dh), BF16),
            pltpu.VMEM((n, dh), F32),
            pltpu.VMEM((nc, 2 * dh, dh), F32),
            pltpu.VMEM((nc, dh, 2 * dh), BF16),
            pltpu.VMEM((chunk, chunk), F32),
            pltpu.VMEM((4, chunk, dh), F32),
            pltpu.VMEM((2, dh, dh), F32),
        ],
        compiler_params=_cparams("arbitrary", "arbitrary"), name="retention",
    )(*args)


def _rope_tables(n, dh):
    rows = n // GRID_W
    row = jnp.repeat(jnp.arange(rows, dtype=F32), GRID_W)
    colp = jnp.tile(jnp.arange(GRID_W, dtype=F32), rows)
    n_freq = dh // 4
    inv_freq = ROPE_BASE ** (-jnp.arange(n_freq, dtype=F32) / n_freq)
    ang = jnp.stack([row[:, None] * inv_freq, colp[:, None] * inv_freq], axis=1)
    ang = jnp.broadcast_to(ang[:, :, None, :], (rows * GRID_W, 2, 2, n_freq)).reshape(rows * GRID_W, dh)
    return jnp.cos(ang), jnp.sin(ang)


def _out_proj_body(n_exp, n_ctx_tiles, *refs):
    if n_ctx_tiles:
        c1_ref, c2_ref = refs[:2]
        refs = refs[2:]
    a1_ref, a2_ref, w_ref, x_ref, g_ref, nw_ref, sc_ref, sh_ref = refs[:8]
    refs = refs[8:]
    if n_exp:
        wr_ref, br_ref = refs[:2]
        refs = refs[2:]
    o_ref, h_ref = refs[:2]
    a1 = a1_ref[...]
    a2 = a2_ref[...]
    if n_ctx_tiles:
        is_ctx = pl.program_id(0) < n_ctx_tiles
        a1 = jnp.where(is_ctx, c1_ref[...], a1)
        a2 = jnp.where(is_ctx, c2_ref[...], a2)
    k1 = a1.shape[1]
    y = jnp.dot(a1, w_ref[0, pl.ds(0, k1), :], preferred_element_type=F32)
    y = y + jnp.dot(a2, w_ref[0, pl.ds(k1, a2.shape[1]), :], preferred_element_type=F32)
    x_new = x_ref[...] + g_ref[0] * y
    o_ref[...] = x_new
    h = _modulated_norm(x_new, nw_ref[...], sc_ref[0], sh_ref[0])
    h_ref[...] = h.astype(h_ref.dtype)
    if n_exp:
        idx, wgt = _router_top2(h, wr_ref[...], br_ref[...], n_exp)
        refs[2][...] = idx
        refs[3][...] = wgt


def _out_proj_residual(ctx_pair, lat_pair, w, layer, x, gate, geom, norm, h_dtype, router=None):
    n_ctx_rows, seq, nb = geom
    nw, sc, sh = norm
    a1, a2 = lat_pair
    k1 = a1.shape[1]
    k2 = a2.shape[1]
    d = w.shape[2]
    first_row = n_ctx_rows if ctx_pair is None else 0
    rows = x.shape[0] - first_row
    tm = _tile(512, n_ctx_rows, seq)
    t0 = first_row // tm
    n_ctx_tiles = 0 if ctx_pair is None else n_ctx_rows // tm
    mrow = _mod_row_fn(tm, first_row, n_ctx_rows, seq, nb)
    mod_spec = pl.BlockSpec((1, 1, d), lambda i: (mrow(i), 0, 0))
    in_specs = []
    args = []
    if ctx_pair is not None:
        in_specs += [pl.BlockSpec((tm, k1), lambda i: (jnp.minimum(i, n_ctx_tiles - 1), 0)),
                     pl.BlockSpec((tm, k2), lambda i: (jnp.minimum(i, n_ctx_tiles - 1), 0))]
        args += list(ctx_pair)
    in_specs += [
        pl.BlockSpec((tm, k1), lambda i: (jnp.maximum(i - n_ctx_tiles, 0), 0)),
        pl.BlockSpec((tm, k2), lambda i: (jnp.maximum(i - n_ctx_tiles, 0), 0)),
        pl.BlockSpec((1, k1 + k2, d), lambda i: (layer, 0, 0)),
        pl.BlockSpec((tm, d), lambda i: (i + t0, 0)),
        mod_spec,
        pl.BlockSpec((1, d), lambda i: (0, 0)),
        mod_spec,
        mod_spec,
    ]
    args += [a1, a2, w, x, gate, nw.reshape(1, d), sc, sh]
    h_rows = rows
    th = 0
    out_specs = [pl.BlockSpec((tm, d), lambda i: (i + t0, 0)), pl.BlockSpec((tm, d), lambda i: (i + th, 0))]
    out_shape = [jax.ShapeDtypeStruct(x.shape, F32), jax.ShapeDtypeStruct((h_rows, d), h_dtype)]
    n_exp = 0
    if router is not None:
        wr, br = router
        n_exp = wr.shape[1]
        in_specs += [pl.BlockSpec((d, LANES), lambda i: (0, 0)), pl.BlockSpec((1, LANES), lambda i: (0, 0))]
        args += [jnp.zeros((d, LANES), F32).at[:, :n_exp].set(wr), jnp.zeros((1, LANES), F32).at[0, :n_exp].set(br)]
        lane_spec = pl.BlockSpec((tm, LANES), lambda i: (i + th, 0))
        out_specs += [lane_spec, lane_spec]
        out_shape += [jax.ShapeDtypeStruct((h_rows, LANES), I32), jax.ShapeDtypeStruct((h_rows, LANES), F32)]
    x_index = 3 if ctx_pair is None else 5
    outs = pl.pallas_call(
        functools.partial(_out_proj_body, n_exp, n_ctx_tiles), grid=(rows // tm,),
        in_specs=in_specs, out_specs=out_specs, out_shape=out_shape,
        input_output_aliases={x_index: 0},
        compiler_params=_cparams("parallel"), name="out_proj",
    )(*args)
    return outs[0], tuple(outs[1:])


def _swiglu_partial(xb, w1, w3, w2):
    h1 = jnp.dot(xb, w1, preferred_element_type=F32)
    h3 = jnp.dot(xb, w3, preferred_element_type=F32)
    t = (h1 * _sigmoid(h1) * h3).astype(BF16)
    return jnp.dot(t, w2, preferred_element_type=F32)


def _ffn_dense_body(h_ref, w1_ref, w3_ref, w2_ref, x_ref, g_ref, nw_ref, sc_ref, sh_ref,
                    o_ref, hn_ref, acc_ref):
    f = pl.program_id(1)

    @pl.when(f == 0)
    def _():
        acc_ref[...] = jnp.zeros_like(acc_ref)

    acc_ref[...] += _swiglu_partial(h_ref[...], w1_ref[0], w3_ref[0], w2_ref[0])

    @pl.when(f == pl.num_programs(1) - 1)
    def _():
        x_new = x_ref[...] + g_ref[0] * acc_ref[...]
        o_ref[...] = x_new
        hn_ref[...] = _modulated_norm(x_new, nw_ref[...], sc_ref[0], sh_ref[0]).astype(hn_ref.dtype)


def _ffn_dense(h, w1, w3, w2, layer, x, gate, geom, first_row, next_norm):
    n_ctx_rows, seq, nb = geom
    rows, d = h.shape
    fdim = w1.shape[2]
    tm = _tile(512, rows, first_row if first_row else rows, n_ctx_rows, seq)
    tf = _tile(512, fdim)
    t0 = first_row // tm
    mrow = _mod_row_fn(tm, first_row, n_ctx_rows, seq, nb)
    mod_spec = pl.BlockSpec((1, 1, d), lambda i, f: (mrow(i), 0, 0))
    nw, sc, sh = next_norm
    return pl.pallas_call(
        _ffn_dense_body, grid=(rows // tm, fdim // tf),
        in_specs=[
            pl.BlockSpec((tm, d), lambda i, f: (i, 0)),
            pl.BlockSpec((1, d, tf), lambda i, f: (layer, 0, f)),
            pl.BlockSpec((1, d, tf), lambda i, f: (layer, 0, f)),
            pl.BlockSpec((1, tf, d), lambda i, f: (layer, f, 0)),
            pl.BlockSpec((tm, d), lambda i, f: (i + t0, 0)),
            mod_spec,
            pl.BlockSpec((1, d), lambda i, f: (0, 0)),
            mod_spec,
            mod_spec,
        ],
        out_specs=[pl.BlockSpec((tm, d), lambda i, f: (i + t0, 0)),
                   pl.BlockSpec((tm, d), lambda i, f: (i, 0))],
        out_shape=[jax.ShapeDtypeStruct(x.shape, F32), jax.ShapeDtypeStruct((rows, d), BF16)],
        scratch_shapes=[pltpu.VMEM((tm, d), F32)],
        input_output_aliases={4: 0},
        compiler_params=_cparams("parallel", "arbitrary"), name="ffn_dense",
    )(h, w1, w3, w2, x, gate, nw.reshape(1, d), sc, sh)


DMA_ISSUE_UNROLL = 8


def _ffn_grouped_body(n_tiles, tm, te_ref, nv_ref, src0_ref, srcn_ref, dstp_ref, dstl_ref,
                      h_hbm, w1_ref, w3_ref, w2_ref, yk_hbm,
                      xg_ref, xb_ref, acc_ref, stage_ref, gsem, ssem):
    i = pl.program_id(0)
    f = pl.program_id(1)
    nf = pl.num_programs(1)
    per = tm // nf
    live = i < nv_ref[0]
    slot = i % 2
    nslot = 1 - slot

    def gather_row(src_ref, r, to_slot):
        return pltpu.make_async_copy(h_hbm.at[pl.ds(src_ref[0, 0, r], 1), :],
                                     xg_ref.at[to_slot, pl.ds(r, 1), :], gsem.at[to_slot])

    def scatter_row(dst_ref, r):
        return pltpu.make_async_copy(stage_ref.at[pl.ds(r, 1), :],
                                     yk_hbm.at[pl.ds(dst_ref[0, 0, r], 1), :], ssem)

    def gather_tile_wait(at_slot):
        pltpu.make_async_copy(h_hbm.at[pl.ds(0, tm), :], xg_ref.at[at_slot], gsem.at[at_slot]).wait()

    def scatter_tile_wait():
        pltpu.make_async_copy(stage_ref, yk_hbm.at[pl.ds(0, tm), :], ssem).wait()

    @pl.when((i == 0) & (f == 0))
    def _():
        def issue(j, carry):
            gather_row(src0_ref, j, 0).start()
            return carry

        lax.fori_loop(0, tm, issue, 0, unroll=DMA_ISSUE_UNROLL)
        stage_ref[...] = jnp.zeros_like(stage_ref)

    @pl.when(f == 0)
    def _():
        gather_tile_wait(slot)
        xb_ref[...] = xg_ref[slot].astype(BF16)
        acc_ref[...] = jnp.zeros_like(acc_ref)

    def issue_step_copies():
        base = f * per
        for j in range(per):
            gather_row(srcn_ref, base + j, nslot).start()
            scatter_row(dstp_ref, base + j).start()

    @pl.when(live)
    def _():
        issue_step_copies()
        acc_ref[...] += _swiglu_partial(xb_ref[...], w1_ref[0, 0], w3_ref[0, 0], w2_ref[0, 0])

    @pl.when(jnp.logical_not(live))
    def _():
        issue_step_copies()

    @pl.when(f == nf - 1)
    def _():
        scatter_tile_wait()
        stage_ref[...] = acc_ref[...]

    @pl.when((f == nf - 1) & (i == n_tiles - 1))
    def _():
        def issue(j, carry):
            scatter_row(dstl_ref, j).start()
            return carry

        lax.fori_loop(0, tm, issue, 0, unroll=DMA_ISSUE_UNROLL)
        scatter_tile_wait()
        gather_tile_wait(nslot)


def _ffn_grouped(h, src, dst, tile_expert, n_live, w1, w3, w2, layer, tm, yk_rows):
    n_tiles = src.shape[0]
    d = h.shape[1]
    fdim = w1.shape[3]
    tf = _tile(512, fdim)
    nf = fdim // tf
    assert tm % nf == 0

    def wcol(i, f, te, nv):
        return jnp.where(i < nv[0], f, nf - 1)

    def smem_tile(index_map):
        return pl.BlockSpec((1, 1, tm), index_map, memory_space=pltpu.SMEM)

    grid_spec = pltpu.PrefetchScalarGridSpec(
        num_scalar_prefetch=2, grid=(n_tiles, nf),
        in_specs=[
            smem_tile(lambda i, f, te, nv: (0, 0, 0)),
            smem_tile(lambda i, f, te, nv: (jnp.minimum(i + 1, n_tiles - 1), 0, 0)),
            smem_tile(lambda i, f, te, nv: (i, 0, 0)),
            smem_tile(lambda i, f, te, nv: (n_tiles, 0, 0)),
            pl.BlockSpec(memory_space=pl.ANY),
            pl.BlockSpec((1, 1, d, tf), lambda i, f, te, nv: (layer, te[i], 0, wcol(i, f, te, nv))),
            pl.BlockSpec((1, 1, d, tf), lambda i, f, te, nv: (layer, te[i], 0, wcol(i, f, te, nv))),
            pl.BlockSpec((1, 1, tf, d), lambda i, f, te, nv: (layer, te[i], wcol(i, f, te, nv), 0)),
        ],
        out_specs=pl.BlockSpec(memory_space=pl.ANY),
        scratch_shapes=[
            pltpu.VMEM((2, tm, d), F32),
            pltpu.VMEM((tm, d), BF16),
            pltpu.VMEM((tm, d), F32),
            pltpu.VMEM((tm, d), F32),
            pltpu.SemaphoreType.DMA((2,)),
            pltpu.SemaphoreType.DMA(()),
        ],
    )
    return pl.pallas_call(
        functools.partial(_ffn_grouped_body, n_tiles, tm), grid_spec=grid_spec,
        out_shape=jax.ShapeDtypeStruct((yk_rows, d), F32),
        compiler_params=_cparams("arbitrary", "arbitrary"), name="ffn_grouped",
    )(tile_expert, n_live, src, src, dst, dst, h, w1, w3, w2)


def _combine_body(modulated, *refs):
    y_refs = refs[:TOP_K]
    refs = refs[TOP_K:]
    if modulated:
        wgt_ref, x_ref, g_ref, nw_ref, sc_ref, sh_ref, o_ref, hn_ref = refs
    else:
        wgt_ref, x_ref, g_ref, nw_ref, o_ref, hn_ref = refs
    w = wgt_ref[...]
    lane = lax.broadcasted_iota(I32, w.shape, 1)
    y = jnp.zeros(x_ref.shape, F32)
    for k in range(TOP_K):
        wk = jnp.sum(jnp.where(lane == k, w, 0.0), axis=-1, keepdims=True)
        y = y + wk * y_refs[k][...]
    x_new = x_ref[...] + g_ref[0] * y
    o_ref[...] = x_new
    if modulated:
        hn = _modulated_norm(x_new, nw_ref[...], sc_ref[0], sh_ref[0])
    else:
        hn = _modulated_norm(x_new, nw_ref[...], 0.0, 0.0)
    hn_ref[...] = hn.astype(hn_ref.dtype)


def _combine_residual(yk, wgt, x, gate, geom, first_row, next_norm):
    n_ctx_rows, seq, nb = geom
    tokens = wgt.shape[0]
    d = x.shape[1]
    tt = _tile(256, tokens, first_row if first_row else tokens, n_ctx_rows, seq)
    t0 = first_row // tt
    mrow = _mod_row_fn(tt, first_row, n_ctx_rows, seq, nb)
    mod_spec = pl.BlockSpec((1, 1, d), lambda i: (mrow(i), 0, 0))
    per_slot = tokens // tt
    slot_specs = [pl.BlockSpec((tt, d), functools.partial(lambda k, i: (i + k * per_slot, 0), k))
                  for k in range(TOP_K)]
    modulated = len(next_norm) == 3
    norm_specs = [pl.BlockSpec((1, d), lambda i: (0, 0))] + ([mod_spec, mod_spec] if modulated else [])
    norm_args = [next_norm[0].reshape(1, d)] + list(next_norm[1:])
    return pl.pallas_call(
        functools.partial(_combine_body, modulated), grid=(per_slot,),
        in_specs=slot_specs + [
            pl.BlockSpec((tt, LANES), lambda i: (i, 0)),
            pl.BlockSpec((tt, d), lambda i: (i + t0, 0)),
            mod_spec,
        ] + norm_specs,
        out_specs=[pl.BlockSpec((tt, d), lambda i: (i + t0, 0)), pl.BlockSpec((tt, d), lambda i: (i, 0))],
        out_shape=[jax.ShapeDtypeStruct(x.shape, F32),
                   jax.ShapeDtypeStruct((tokens, d), BF16 if modulated else F32)],
        input_output_aliases={TOP_K + 1: 0},
        compiler_params=_cparams("parallel"), name="moe_combine",
    )(*([yk] * TOP_K), wgt, x, gate, *norm_args)


def _moe(h, idx, wgt, w1, w3, w2, layer, x, gate, geom, first_row, next_norm):
    tokens, d = h.shape
    n_exp = w1.shape[1]
    tm = _tile(512, tokens)
    n_assign = TOP_K * tokens
    n_tiles = n_assign // tm + n_exp
    e_flat = idx[:, :TOP_K].reshape(n_assign)
    onehot = (e_flat[:, None] == jnp.arange(n_exp, dtype=I32)[None, :]).astype(I32)
    csum = jnp.cumsum(onehot, axis=0)
    rank = jnp.sum(csum * onehot, axis=1) - 1
    counts = csum[-1]
    padded = ((counts + tm - 1) // tm) * tm
    ends = jnp.cumsum(padded)
    starts = ends - padded
    pos = jnp.sum(onehot * starts[None, :], axis=1) + rank
    n_live = (ends[-1] // tm).astype(I32).reshape(1)
    tile_start = jnp.arange(n_tiles, dtype=I32) * tm
    tile_expert = jnp.sum((tile_start[:, None] >= ends[None, :]).astype(I32), axis=1)
    last_expert = jnp.max(jnp.where(counts > 0, jnp.arange(n_exp, dtype=I32), 0))
    tile_expert = jnp.minimum(tile_expert, last_expert).astype(I32)
    n_rows = n_tiles * tm
    assign_of_row = jnp.full((n_rows,), n_assign, I32).at[pos].set(jnp.arange(n_assign, dtype=I32))
    is_pad = assign_of_row >= n_assign
    src = jnp.where(is_pad, 0, assign_of_row // TOP_K)
    pad_ordinal = jnp.cumsum(is_pad.astype(I32)) - 1
    yk_row = jnp.where(is_pad, n_assign + pad_ordinal,
                       (assign_of_row % TOP_K) * tokens + assign_of_row // TOP_K)
    spare = n_rows + jnp.arange(tm, dtype=I32)
    dst = jnp.concatenate([spare, yk_row])

    yk = _ffn_grouped(h, src.reshape(n_tiles, 1, tm), dst.reshape(n_tiles + 1, 1, tm), tile_expert, n_live,
                      w1, w3, w2, layer, tm, n_rows + tm)
    return _combine_residual(yk, wgt, x, gate, geom, first_row, next_norm)


def _final_norm_body(x_ref, w_ref, o_ref):
    x = x_ref[...]
    ms = jnp.mean(x * x, axis=-1, keepdims=True)
    o_ref[...] = x * lax.rsqrt(ms + NORM_EPS) * w_ref[...]


def _final_norm(x, w, first_row):
    r, d = x.shape
    rows = r - first_row
    tm = _tile(256, rows, first_row)
    t0 = first_row // tm
    return pl.pallas_call(
        _final_norm_body, grid=(rows // tm,),
        in_specs=[pl.BlockSpec((tm, d), lambda i: (i + t0, 0)),
                  pl.BlockSpec((1, d), lambda i: (0, 0))],
        out_specs=pl.BlockSpec((tm, d), lambda i: (i, 0)),
        out_shape=jax.ShapeDtypeStruct((rows, d), F32),
        compiler_params=_cparams("parallel"), name="final_norm",
    )(x, w.reshape(1, d))


def kernel(x, c, ctx, c_ctx, w_mod, b_mod, norm1_w, norm2_w, w_in, conv_w, conv_b, conv_ln_w,
           conv_ln_b, ret_decay_logit, ret_gn_w, ret_gn_b, w_out, ffn_w1, ffn_w3, ffn_w2,
           moe_router, moe_router_b, moe_w1, moe_w3, moe_w2, final_w):
    nb, seq, d = x.shape
    n_ctx = ctx.shape[1]
    depth = w_mod.shape[0]
    cw = conv_w.shape[2]
    n_heads = ret_decay_logit.shape[2]
    dh = ret_gn_w.shape[1] // n_heads
    n_ctx_rows = nb * n_ctx
    geom = (n_ctx_rows, seq, nb)
    q_col0 = 2 * cw // LANES

    xs = jnp.concatenate([ctx.reshape(n_ctx_rows, d), x.reshape(nb * seq, d)], axis=0)

    rm = -(-(nb + 1) // SUBLANES) * SUBLANES
    c_all = jnp.zeros((rm, d), F32).at[:nb].set(c).at[nb].set(c_ctx)
    mods = _modulation(c_all, w_mod, b_mod).reshape(depth, rm, 6, 1, d)

    rope_tabs = _rope_tables(seq, dh)
    w_in_b = w_in.astype(BF16)
    w_out_b = w_out.astype(BF16)
    ffn_b = tuple(w.astype(BF16) for w in (ffn_w1, ffn_w3, ffn_w2))
    moe_b = tuple(w.astype(BF16) for w in (moe_w1, moe_w3, moe_w2))

    h = _norm_mod(xs, norm1_w[0], mods[0, :, 1], mods[0, :, 0], geom, 0, BF16)
    out = None
    for i in range(depth):
        last = i == depth - 1
        j = i // 2
        sh1, sc1, g1, sh2, sc2, g2 = (mods[i, :, m] for m in range(6))
        nxt = (norm1_w[i], sc1, sh1) if last else (norm1_w[i + 1], mods[i + 1, :, 1], mods[i + 1, :, 0])
        moe_layer = i % 2 == 1

        u = _matmul(h, w_in_b, i, BF16)

        ret_ctx, ctx_state = _retention(u, 0, n_ctx, nb, q_col0, ret_decay_logit[i],
                                        ret_gn_w[i], ret_gn_b[i], None, None)
        ret_lat, _ = _retention(u, n_ctx_rows, seq, nb, q_col0, ret_decay_logit[i],
                                ret_gn_w[i], ret_gn_b[i], rope_tabs, ctx_state)
        conv_lat = _conformer_conv(u, n_ctx_rows, seq, nb, conv_w[i], conv_b[i],
                                   conv_ln_w[i], conv_ln_b[i])
        first_row = n_ctx_rows if last else 0
        norm2 = (norm2_w[i], sc2, sh2)
        router = (moe_router[j], moe_router_b[j]) if moe_layer else None
        h2_dtype = F32 if moe_layer else BF16
        ctx_pair = None
        if not last:
            conv_ctx = _conformer_conv(u, 0, n_ctx, nb, conv_w[i], conv_b[i],
                                       conv_ln_w[i], conv_ln_b[i])
            ctx_pair = (conv_ctx, ret_ctx)
        xs, h2_parts = _out_proj_residual(ctx_pair, (conv_lat, ret_lat), w_out_b, i, xs, g1, geom,
                                          norm2, h2_dtype, router=router)

        if moe_layer:
            h2, idx, wgt = h2_parts
            xs, hn = _moe(h2, idx, wgt, *moe_b, j, xs, g2, geom, first_row, (final_w,) if last else nxt)
            if last:
                out = hn
        else:
            xs, hn = _ffn_dense(h2_parts[0], *ffn_b, j, xs, g2, geom, first_row, nxt)
        h = hn

    if out is None:
        out = _final_norm(xs, final_w, n_ctx_rows)
    return out.reshape(nb, seq, d)
```

```python
import functools

import jax
import jax.numpy as jnp
from jax import lax
from jax.experimental import pallas as pl
from jax.experimental.pallas import tpu as pltpu

F32 = jnp.float32
BF16 = jnp.bfloat16
I32 = jnp.int32

GRID_W = 64
ROPE_BASE = 10000.0
NORM_EPS = 1e-6
TOP_K = 2
LANES = 128
SUBLANES = 8
VMEM_LIMIT_BYTES = 56 * 1024 * 1024


def _cparams(*sem):
    return pltpu.CompilerParams(dimension_semantics=sem, vmem_limit_bytes=VMEM_LIMIT_BYTES)


def _tile(pref, *dims):
    t = pref
    while any(d % t for d in dims):
        t //= 2
    return t


def _sigmoid(v):
    return 1.0 / (1.0 + jnp.exp(-v))


def _log_sigmoid(v):
    return jnp.minimum(v, 0.0) - jnp.log1p(jnp.exp(-jnp.abs(v)))


def _mod_body(c_ref, w_ref, b_ref, o_ref):
    c = c_ref[...]
    s = (c * _sigmoid(c)).astype(BF16)
    o_ref[0] = jnp.dot(s, w_ref[0].astype(BF16), preferred_element_type=F32) + b_ref[0]


def _modulation(c_all, w_mod, b_mod):
    depth, d, n6 = w_mod.shape
    rm = c_all.shape[0]
    tn = _tile(1024, n6)
    return pl.pallas_call(
        _mod_body,
        grid=(depth, n6 // tn),
        in_specs=[
            pl.BlockSpec((rm, d), lambda l, j: (0, 0)),
            pl.BlockSpec((1, d, tn), lambda l, j: (l, 0, j)),
            pl.BlockSpec((1, 1, tn), lambda l, j: (l, 0, j)),
        ],
        out_specs=pl.BlockSpec((1, rm, tn), lambda l, j: (l, 0, j)),
        out_shape=jax.ShapeDtypeStruct((depth, rm, n6), F32),
        compiler_params=_cparams("parallel", "parallel"),
        name="modulation",
    )(c_all, w_mod, b_mod.reshape(depth, 1, n6))


def _mod_row_fn(tm, first_row, n_ctx_rows, seq, nb):
    t0 = first_row // tm
    n_ctx_tiles = n_ctx_rows // tm
    per_seq = seq // tm

    def f(i):
        g = i + t0
        return jnp.where(g < n_ctx_tiles, nb, (g - n_ctx_tiles) // per_seq)

    return f


def _modulated_norm(x, nw, sc, sh):
    ms = jnp.mean(x * x, axis=-1, keepdims=True)
    y = x * lax.rsqrt(ms + NORM_EPS) * nw
    return y * (1.0 + sc) + sh


def _norm_mod_body(x_ref, nw_ref, sc_ref, sh_ref, h_ref):
    h = _modulated_norm(x_ref[...], nw_ref[...], sc_ref[0], sh_ref[0])
    h_ref[...] = h.astype(h_ref.dtype)


def _router_top2(h, wr, br, n_exp):
    logits = jnp.dot(h, wr, preferred_element_type=F32, precision=lax.Precision.HIGHEST) + br
    lane = lax.broadcasted_iota(I32, logits.shape, 1)
    neg = jnp.float32(-jnp.inf)
    logits = jnp.where(lane < n_exp, logits, neg)
    m1 = jnp.max(logits, axis=-1, keepdims=True)
    i1 = jnp.min(jnp.where(logits == m1, lane, LANES), axis=-1, keepdims=True)
    rest = jnp.where(lane == i1, neg, logits)
    m2 = jnp.max(rest, axis=-1, keepdims=True)
    i2 = jnp.min(jnp.where(rest == m2, lane, LANES), axis=-1, keepdims=True)
    e = jnp.exp(m2 - m1)
    w1 = 1.0 / (1.0 + e)
    w2 = e / (1.0 + e)
    idx = jnp.where(lane == 0, i1, jnp.where(lane == 1, i2, 0))
    wgt = jnp.where(lane == 0, w1, jnp.where(lane == 1, w2, 0.0))
    return idx, wgt


def _router_body(n_exp, h_ref, wr_ref, br_ref, idx_ref, wgt_ref):
    idx, wgt = _router_top2(h_ref[...], wr_ref[...], br_ref[...], n_exp)
    idx_ref[...] = idx
    wgt_ref[...] = wgt


def _router(h, wr, br):
    rows, d = h.shape
    n_exp = wr.shape[1]
    tm = _tile(256, rows)
    wr_p = jnp.zeros((d, LANES), F32).at[:, :n_exp].set(wr)
    br_p = jnp.zeros((1, LANES), F32).at[0, :n_exp].set(br)
    lane_spec = pl.BlockSpec((tm, LANES), lambda i: (i, 0))
    return pl.pallas_call(
        functools.partial(_router_body, n_exp), grid=(rows // tm,),
        in_specs=[pl.BlockSpec((tm, d), lambda i: (i, 0)),
                  pl.BlockSpec((d, LANES), lambda i: (0, 0)),
                  pl.BlockSpec((1, LANES), lambda i: (0, 0))],
        out_specs=[lane_spec, lane_spec],
        out_shape=[jax.ShapeDtypeStruct((rows, LANES), I32), jax.ShapeDtypeStruct((rows, LANES), F32)],
        compiler_params=_cparams("parallel"), name="router",
    )(h, wr_p, br_p)


def _norm_mod(x, nw, sc, sh, geom, first_row, out_dtype):
    n_ctx_rows, seq, nb = geom
    r, d = x.shape
    rows = r - first_row
    tm = _tile(256, rows, first_row if first_row else rows, n_ctx_rows, seq)
    t0 = first_row // tm
    mrow = _mod_row_fn(tm, first_row, n_ctx_rows, seq, nb)
    in_specs = [
        pl.BlockSpec((tm, d), lambda i: (i + t0, 0)),
        pl.BlockSpec((1, d), lambda i: (0, 0)),
        pl.BlockSpec((1, 1, d), lambda i: (mrow(i), 0, 0)),
        pl.BlockSpec((1, 1, d), lambda i: (mrow(i), 0, 0)),
    ]
    args = [x, nw.reshape(1, d), sc, sh]
    return pl.pallas_call(
        _norm_mod_body, grid=(rows // tm,), in_specs=in_specs,
        out_specs=pl.BlockSpec((tm, d), lambda i: (i, 0)),
        out_shape=jax.ShapeDtypeStruct((rows, d), out_dtype),
        compiler_params=_cparams("parallel"), name="norm_mod",
    )(*args)


def _matmul_body(a_ref, w_ref, o_ref):
    o_ref[...] = jnp.dot(a_ref[...], w_ref[0], preferred_element_type=F32).astype(o_ref.dtype)


def _matmul(a, w, layer, out_dtype):
    m, k = a.shape
    n = w.shape[2]
    tm = _tile(1024, m)
    tn = _tile(1024, n)
    return pl.pallas_call(
        _matmul_body, grid=(m // tm, n // tn),
        in_specs=[pl.BlockSpec((tm, k), lambda i, j: (i, 0)),
                  pl.BlockSpec((1, k, tn), lambda i, j: (layer, 0, j))],
        out_specs=pl.BlockSpec((tm, tn), lambda i, j: (i, j)),
        out_shape=jax.ShapeDtypeStruct((m, n), out_dtype),
        compiler_params=_cparams("parallel", "parallel"), name="in_proj",
    )(a, w)


CONV_ROWS = 64
CONV_PAD = 16


def _conv_body(n, kw, a_ref, b_ref, cw_ref, cb_ref, lw_ref, lb_ref, o_ref, g_ref, acc_ref):
    cw = a_ref.shape[1]
    half = kw // 2
    zeros = jnp.zeros((CONV_PAD, cw), F32)
    g_ref[pl.ds(0, CONV_PAD), :] = zeros
    g_ref[pl.ds(CONV_PAD + n, CONV_PAD), :] = zeros

    def glu_step(i, carry):
        r = pl.multiple_of(i * CONV_ROWS, CONV_ROWS)
        a = a_ref[pl.ds(r, CONV_ROWS), :].astype(F32)
        b = b_ref[pl.ds(r, CONV_ROWS), :].astype(F32)
        g_ref[pl.ds(r + CONV_PAD, CONV_ROWS), :] = a * _sigmoid(b)
        return carry

    lax.fori_loop(0, n // CONV_ROWS, glu_step, 0)

    def conv_step(i, carry):
        r = pl.multiple_of(i * CONV_ROWS, CONV_ROWS)
        for c in range(cw // LANES):
            lanes = pl.ds(c * LANES, LANES)
            win = g_ref[pl.ds(r, CONV_ROWS + 2 * CONV_PAD), lanes]
            out = jnp.zeros((CONV_ROWS, LANES), F32) + cb_ref[:, lanes]
            for s in range(SUBLANES):
                part = None
                for q in range(2 * CONV_PAD // SUBLANES):
                    k = SUBLANES * q + s - (CONV_PAD - half)
                    if 0 <= k < kw:
                        term = (win[SUBLANES * q:SUBLANES * q + CONV_ROWS + SUBLANES, :]
                                * cw_ref[pl.ds(k, 1), lanes])
                        part = term if part is None else part + term
                out = out + part[s:s + CONV_ROWS, :]
            acc_ref[:, lanes] = out
        acc = acc_ref[...]
        mu = jnp.mean(acc, axis=-1, keepdims=True)
        cen = acc - mu
        var = jnp.mean(cen * cen, axis=-1, keepdims=True)
        y = cen * lax.rsqrt(var + NORM_EPS) * lw_ref[...] + lb_ref[...]
        o_ref[pl.ds(r, CONV_ROWS), :] = (y * _sigmoid(y)).astype(o_ref.dtype)
        return carry

    lax.fori_loop(0, n // CONV_ROWS, conv_step, 0)


def _conformer_conv(u, first_row, n, nb, conv_w, conv_b, ln_w, ln_b):
    kw, cw = conv_w.shape
    assert kw // 2 < CONV_PAD and n % CONV_ROWS == 0 and first_row % n == 0
    b0 = first_row // n
    return pl.pallas_call(
        functools.partial(_conv_body, n, kw),
        grid=(nb,),
        in_specs=[
            pl.BlockSpec((n, cw), lambda b: (b + b0, 0)),
            pl.BlockSpec((n, cw), lambda b: (b + b0, 1)),
            pl.BlockSpec((kw, cw), lambda b: (0, 0)),
            pl.BlockSpec((1, cw), lambda b: (0, 0)),
            pl.BlockSpec((1, cw), lambda b: (0, 0)),
            pl.BlockSpec((1, cw), lambda b: (0, 0)),
        ],
        out_specs=pl.BlockSpec((n, cw), lambda b: (b, 0)),
        out_shape=jax.ShapeDtypeStruct((nb * n, cw), BF16),
        scratch_shapes=[pltpu.VMEM((n + 2 * CONV_PAD, cw), F32), pltpu.VMEM((CONV_ROWS, cw), F32)],
        compiler_params=_cparams("parallel"), name="conformer_conv",
    )(u, u, conv_w, conv_b.reshape(1, cw), ln_w.reshape(1, cw), ln_b.reshape(1, cw))


RET_CHUNK_ROWS = 256
RET_UNROLL = 4


def _retention_body(n, chunk, rope, has_init, k_scale, *refs):
    dl_ref, q_ref, k_ref, v_ref, gate_ref, gw_ref, gb_ref = refs[:7]
    pos = 7
    if rope:
        cos_ref, sin_ref = refs[pos:pos + 2]
        pos += 2
    if has_init:
        s0_ref = refs[pos]
        pos += 1
    o_ref, sout_ref, qs_ref, acc_ref, u_ref, st_ref, decay_ref, vec_ref, cd_ref = refs[pos:pos + 9]
    dh = q_ref.shape[1]
    nc = n // chunk
    h = pl.program_id(0)
    nt = (((1,), (1,)), ((), ()))
    tn = (((0,), (0,)), ((), ()))

    def rows(c):
        return pl.ds(pl.multiple_of(c * chunk, chunk), chunk)

    @pl.when(pl.program_id(1) == 0)
    def _():
        def log_decay(direction, shape):
            return _log_sigmoid(jnp.full(shape, dl_ref[direction, h], F32))

        ii = lax.broadcasted_iota(I32, (chunk, chunk), 0)
        jj = lax.broadcasted_iota(I32, (chunk, chunk), 1)
        diff = (ii - jj).astype(F32)
        decay_ref[...] = jnp.where(
            diff >= 0.0,
            jnp.exp(jnp.maximum(diff, 0.0) * log_decay(0, (chunk, chunk))),
            jnp.exp(jnp.maximum(-diff, 0.0) * log_decay(1, (chunk, chunk))))
        idx = lax.broadcasted_iota(I32, (chunk, dh), 0).astype(F32)
        lg_f = log_decay(0, (chunk, dh))
        lg_b = log_decay(1, (chunk, dh))
        vec_ref[0] = jnp.exp((idx + 1.0) * lg_f)
        vec_ref[1] = jnp.exp((chunk - 1.0 - idx) * lg_f)
        vec_ref[2] = jnp.exp((chunk - idx) * lg_b)
        vec_ref[3] = jnp.exp(idx * lg_b)
        cd_ref[0] = jnp.exp(chunk * log_decay(0, (dh, dh)))
        cd_ref[1] = jnp.exp(chunk * log_decay(1, (dh, dh)))

    def intra(c, carry):
        rs = rows(c)
        q = q_ref[rs, :].astype(F32)
        k = k_ref[rs, :].astype(F32)
        if rope:
            lane = lax.broadcasted_iota(I32, (chunk, dh), 1)
            first = (lane % (dh // 2)) < (dh // 4)
            cos = cos_ref[rs, :]
            sin = sin_ref[rs, :]

            def rot(t):
                return jnp.where(first, -pltpu.roll(t, dh - dh // 4, 1), pltpu.roll(t, dh // 4, 1))

            q = q * cos + rot(q) * sin
            k = k * cos + rot(k) * sin
        qb = q.astype(BF16)
        qs_ref[rs, :] = qb
        kf = k * k_scale
        v = v_ref[rs, :]
        s = lax.dot_general(qb, kf.astype(BF16), nt, preferred_element_type=F32)
        acc_ref[rs, :] = jnp.dot((s * decay_ref[...]).astype(BF16), v, preferred_element_type=F32)
        kd = jnp.concatenate([kf * vec_ref[1], kf * vec_ref[3]], axis=1).astype(BF16)
        u_ref[c] = lax.dot_general(kd, v, tn, preferred_element_type=F32)
        return carry

    lax.fori_loop(0, nc, intra, 0, unroll=min(nc, RET_UNROLL))

    if has_init:
        init_f = s0_ref[0, 0, 0]
        init_b = s0_ref[0, 0, 1]
    else:
        init_f = jnp.zeros((dh, dh), F32)
        init_b = jnp.zeros((dh, dh), F32)

    def scan_f(c, state):
        st_ref[c, :, pl.ds(0, dh)] = state.astype(BF16)
        return state * cd_ref[0] + u_ref[c, pl.ds(0, dh), :]

    def scan_b(t, state):
        c = nc - 1 - t
        st_ref[c, :, pl.ds(dh, dh)] = state.astype(BF16)
        return state * cd_ref[1] + u_ref[c, pl.ds(dh, dh), :]

    sout_ref[0, 0, 0] = lax.fori_loop(0, nc, scan_f, init_f)
    sout_ref[0, 0, 1] = lax.fori_loop(0, nc, scan_b, init_b)

    def readout(c, carry):
        rs = rows(c)
        r = jnp.dot(qs_ref[rs, :], st_ref[c], preferred_element_type=F32)
        o = acc_ref[rs, :] + r[:, :dh] * vec_ref[0] + r[:, dh:] * vec_ref[2]
        mu = jnp.mean(o, axis=-1, keepdims=True)
        cen = o - mu
        var = jnp.mean(cen * cen, axis=-1, keepdims=True)
        on = cen * lax.rsqrt(var + NORM_EPS) * gw_ref[...] + gb_ref[...]
        g = gate_ref[rs, :].astype(F32)
        o_ref[rs, :] = (on * (g * _sigmoid(g))).astype(o_ref.dtype)
        return carry

    lax.fori_loop(0, nc, readout, 0, unroll=min(nc, RET_UNROLL))


def _retention(u, first_row, n, nb, col0, decay_logit, gn_w, gn_b, rope_tabs, init_state):
    n_heads = decay_logit.shape[1]
    dh = gn_w.shape[0] // n_heads
    assert dh == LANES and first_row % n == 0
    chunk = _tile(RET_CHUNK_ROWS, n)
    nc = n // chunk
    b0 = first_row // n
    rope = rope_tabs is not None
    has_init = init_state is not None

    def col(kind):
        return lambda h, b: (b + b0, col0 + kind * n_heads + h)

    in_specs = [
        pl.BlockSpec(memory_space=pltpu.SMEM),
        pl.BlockSpec((n, dh), col(0)),
        pl.BlockSpec((n, dh), col(1)),
        pl.BlockSpec((n, dh), col(2)),
        pl.BlockSpec((n, dh), col(3)),
        pl.BlockSpec((1, dh), lambda h, b: (0, h)),
        pl.BlockSpec((1, dh), lambda h, b: (0, h)),
    ]
    args = [decay_logit, u, u, u, u, gn_w.reshape(1, -1), gn_b.reshape(1, -1)]
    if rope:
        in_specs += [pl.BlockSpec((n, dh), lambda h, b: (0, 0))] * 2
        args += list(rope_tabs)
    state_spec = pl.BlockSpec((1, 1, 2, dh, dh), lambda h, b: (b, h, 0, 0, 0))
    if has_init:
        in_specs.append(state_spec)
        args.append(init_state)
    return pl.pallas_call(
        functools.partial(_retention_body, n, chunk, rope, has_init, float(dh) ** -0.5),
        grid=(n_heads, nb),
        in_specs=in_specs,
        out_specs=[pl.BlockSpec((n, dh), lambda h, b: (b, h)), state_spec],
        out_shape=[jax.ShapeDtypeStruct((nb * n, n_heads * dh), BF16),
                   jax.ShapeDtypeStruct((nb, n_heads, 2, dh, dh), F32)],
        scratch_shapes=[
            pltpu.VMEM((n, dh), BF16),
            pltpu.VMEM((n, dh), F32),
            pltpu.VMEM((nc, 2 * dh, dh), F32),
            pltpu.VMEM((nc, dh, 2 * dh), BF16),
            pltpu.VMEM((chunk, chunk), F32),
            pltpu.VMEM((4, chunk, dh), F32),
            pltpu.VMEM((2, dh, dh), F32),
        ],
        compiler_params=_cparams("arbitrary", "arbitrary"), name="retention",
    )(*args)


def _rope_tables(n, dh):
    rows = n // GRID_W
    row = jnp.repeat(jnp.arange(rows, dtype=F32), GRID_W)
    colp = jnp.tile(jnp.arange(GRID_W, dtype=F32), rows)
    n_freq = dh // 4
    inv_freq = ROPE_BASE ** (-jnp.arange(n_freq, dtype=F32) / n_freq)
    ang = jnp.stack([row[:, None] * inv_freq, colp[:, None] * inv_freq], axis=1)
    ang = jnp.broadcast_to(ang[:, :, None, :], (rows * GRID_W, 2, 2, n_freq)).reshape(rows * GRID_W, dh)
    return jnp.cos(ang), jnp.sin(ang)


def _out_proj_body(n_ctx_tiles, *refs):
    if n_ctx_tiles:
        c1_ref, c2_ref = refs[:2]
        refs = refs[2:]
    a1_ref, a2_ref, w_ref, x_ref, g_ref, nw_ref, sc_ref, sh_ref, o_ref, h_ref = refs
    a1 = a1_ref[...]
    a2 = a2_ref[...]
    if n_ctx_tiles:
        is_ctx = pl.program_id(0) < n_ctx_tiles
        a1 = jnp.where(is_ctx, c1_ref[...], a1)
        a2 = jnp.where(is_ctx, c2_ref[...], a2)
    k1 = a1.shape[1]
    y = jnp.dot(a1, w_ref[0, pl.ds(0, k1), :], preferred_element_type=F32)
    y = y + jnp.dot(a2, w_ref[0, pl.ds(k1, a2.shape[1]), :], preferred_element_type=F32)
    x_new = x_ref[...] + g_ref[0] * y
    o_ref[...] = x_new
    h = _modulated_norm(x_new, nw_ref[...], sc_ref[0], sh_ref[0])
    h_ref[...] = h.astype(h_ref.dtype)


def _out_proj_residual(ctx_pair, lat_pair, w, layer, x, gate, geom, norm, h_dtype):
    n_ctx_rows, seq, nb = geom
    nw, sc, sh = norm
    a1, a2 = lat_pair
    k1 = a1.shape[1]
    k2 = a2.shape[1]
    d = w.shape[2]
    first_row = n_ctx_rows if ctx_pair is None else 0
    rows = x.shape[0] - first_row
    tm = _tile(512, n_ctx_rows, seq)
    t0 = first_row // tm
    n_ctx_tiles = 0 if ctx_pair is None else n_ctx_rows // tm
    mrow = _mod_row_fn(tm, first_row, n_ctx_rows, seq, nb)
    mod_spec = pl.BlockSpec((1, 1, d), lambda i: (mrow(i), 0, 0))
    in_specs = []
    args = []
    if ctx_pair is not None:
        in_specs += [pl.BlockSpec((tm, k1), lambda i: (jnp.minimum(i, n_ctx_tiles - 1), 0)),
                     pl.BlockSpec((tm, k2), lambda i: (jnp.minimum(i, n_ctx_tiles - 1), 0))]
        args += list(ctx_pair)
    in_specs += [
        pl.BlockSpec((tm, k1), lambda i: (jnp.maximum(i - n_ctx_tiles, 0), 0)),
        pl.BlockSpec((tm, k2), lambda i: (jnp.maximum(i - n_ctx_tiles, 0), 0)),
        pl.BlockSpec((1, k1 + k2, d), lambda i: (layer, 0, 0)),
        pl.BlockSpec((tm, d), lambda i: (i + t0, 0)),
        mod_spec,
        pl.BlockSpec((1, d), lambda i: (0, 0)),
        mod_spec,
        mod_spec,
    ]
    args += [a1, a2, w, x, gate, nw.reshape(1, d), sc, sh]
    x_index = 3 if ctx_pair is None else 5
    return pl.pallas_call(
        functools.partial(_out_proj_body, n_ctx_tiles), grid=(rows // tm,),
        in_specs=in_specs,
        out_specs=[pl.BlockSpec((tm, d), lambda i: (i + t0, 0)), pl.BlockSpec((tm, d), lambda i: (i, 0))],
        out_shape=[jax.ShapeDtypeStruct(x.shape, F32), jax.ShapeDtypeStruct((rows, d), h_dtype)],
        input_output_aliases={x_index: 0},
        compiler_params=_cparams("parallel"), name="out_proj",
    )(*args)


def _swiglu_partial(xb, w1, w3, w2):
    h1 = jnp.dot(xb, w1, preferred_element_type=F32)
    h3 = jnp.dot(xb, w3, preferred_element_type=F32)
    t = (h1 * _sigmoid(h1) * h3).astype(BF16)
    return jnp.dot(t, w2, preferred_element_type=F32)


def _ffn_dense_body(h_ref, w1_ref, w3_ref, w2_ref, x_ref, g_ref, nw_ref, sc_ref, sh_ref,
                    o_ref, hn_ref, acc_ref):
    f = pl.program_id(1)

    @pl.when(f == 0)
    def _():
        acc_ref[...] = jnp.zeros_like(acc_ref)

    acc_ref[...] += _swiglu_partial(h_ref[...], w1_ref[0], w3_ref[0], w2_ref[0])

    @pl.when(f == pl.num_programs(1) - 1)
    def _():
        x_new = x_ref[...] + g_ref[0] * acc_ref[...]
        o_ref[...] = x_new
        hn_ref[...] = _modulated_norm(x_new, nw_ref[...], sc_ref[0], sh_ref[0]).astype(hn_ref.dtype)


def _ffn_dense(h, w1, w3, w2, layer, x, gate, geom, first_row, next_norm):
    n_ctx_rows, seq, nb = geom
    rows, d = h.shape
    fdim = w1.shape[2]
    tm = _tile(512, rows, first_row if first_row else rows, n_ctx_rows, seq)
    tf = _tile(512, fdim)
    t0 = first_row // tm
    mrow = _mod_row_fn(tm, first_row, n_ctx_rows, seq, nb)
    mod_spec = pl.BlockSpec((1, 1, d), lambda i, f: (mrow(i), 0, 0))
    nw, sc, sh = next_norm
    return pl.pallas_call(
        _ffn_dense_body, grid=(rows // tm, fdim // tf),
        in_specs=[
            pl.BlockSpec((tm, d), lambda i, f: (i, 0)),
            pl.BlockSpec((1, d, tf), lambda i, f: (layer, 0, f)),
            pl.BlockSpec((1, d, tf), lambda i, f: (layer, 0, f)),
            pl.BlockSpec((1, tf, d), lambda i, f: (layer, f, 0)),
            pl.BlockSpec((tm, d), lambda i, f: (i + t0, 0)),
            mod_spec,
            pl.BlockSpec((1, d), lambda i, f: (0, 0)),
            mod_spec,
            mod_spec,
        ],
        out_specs=[pl.BlockSpec((tm, d), lambda i, f: (i + t0, 0)),
                   pl.BlockSpec((tm, d), lambda i, f: (i, 0))],
        out_shape=[jax.ShapeDtypeStruct(x.shape, F32), jax.ShapeDtypeStruct((rows, d), BF16)],
        scratch_shapes=[pltpu.VMEM((tm, d), F32)],
        input_output_aliases={4: 0},
        compiler_params=_cparams("parallel", "arbitrary"), name="ffn_dense",
    )(h, w1, w3, w2, x, gate, nw.reshape(1, d), sc, sh)


DMA_ISSUE_UNROLL = 8


def _ffn_grouped_body(n_tiles, tm, te_ref, nv_ref, src0_ref, srcn_ref, dstp_ref, dstl_ref,
                      h_hbm, w1_ref, w3_ref, w2_ref, yk_hbm,
                      xg_ref, xb_ref, acc_ref, stage_ref, gsem, ssem):
    i = pl.program_id(0)
    f = pl.program_id(1)
    nf = pl.num_programs(1)
    per = tm // nf
    live = i < nv_ref[0]
    slot = i % 2
    nslot = 1 - slot

    def gather_row(src_ref, r, to_slot):
        return pltpu.make_async_copy(h_hbm.at[pl.ds(src_ref[0, 0, r], 1), :],
                                     xg_ref.at[to_slot, pl.ds(r, 1), :], gsem.at[to_slot])

    def scatter_row(dst_ref, r):
        return pltpu.make_async_copy(stage_ref.at[pl.ds(r, 1), :],
                                     yk_hbm.at[pl.ds(dst_ref[0, 0, r], 1), :], ssem)

    def gather_tile_wait(at_slot):
        pltpu.make_async_copy(h_hbm.at[pl.ds(0, tm), :], xg_ref.at[at_slot], gsem.at[at_slot]).wait()

    def scatter_tile_wait():
        pltpu.make_async_copy(stage_ref, yk_hbm.at[pl.ds(0, tm), :], ssem).wait()

    @pl.when((i == 0) & (f == 0))
    def _():
        def issue(j, carry):
            gather_row(src0_ref, j, 0).start()
            return carry

        lax.fori_loop(0, tm, issue, 0, unroll=DMA_ISSUE_UNROLL)
        stage_ref[...] = jnp.zeros_like(stage_ref)

    @pl.when(f == 0)
    def _():
        gather_tile_wait(slot)
        xb_ref[...] = xg_ref[slot].astype(BF16)
        acc_ref[...] = jnp.zeros_like(acc_ref)

    def issue_step_copies():
        base = f * per
        for j in range(per):
            gather_row(srcn_ref, base + j, nslot).start()
            scatter_row(dstp_ref, base + j).start()

    @pl.when(live)
    def _():
        issue_step_copies()
        acc_ref[...] += _swiglu_partial(xb_ref[...], w1_ref[0, 0], w3_ref[0, 0], w2_ref[0, 0])

    @pl.when(jnp.logical_not(live))
    def _():
        issue_step_copies()

    @pl.when(f == nf - 1)
    def _():
        scatter_tile_wait()
        stage_ref[...] = acc_ref[...]

    @pl.when((f == nf - 1) & (i == n_tiles - 1))
    def _():
        def issue(j, carry):
            scatter_row(dstl_ref, j).start()
            return carry

        lax.fori_loop(0, tm, issue, 0, unroll=DMA_ISSUE_UNROLL)
        scatter_tile_wait()
        gather_tile_wait(nslot)


def _ffn_grouped(h, src, dst, tile_expert, n_live, w1, w3, w2, layer, tm, yk_rows):
    n_tiles = src.shape[0]
    d = h.shape[1]
    fdim = w1.shape[3]
    tf = _tile(512, fdim)
    nf = fdim // tf
    assert tm % nf == 0

    def wcol(i, f, te, nv):
        return jnp.where(i < nv[0], f, nf - 1)

    def smem_tile(index_map):
        return pl.BlockSpec((1, 1, tm), index_map, memory_space=pltpu.SMEM)

    grid_spec = pltpu.PrefetchScalarGridSpec(
        num_scalar_prefetch=2, grid=(n_tiles, nf),
        in_specs=[
            smem_tile(lambda i, f, te, nv: (0, 0, 0)),
            smem_tile(lambda i, f, te, nv: (jnp.minimum(i + 1, n_tiles - 1), 0, 0)),
            smem_tile(lambda i, f, te, nv: (i, 0, 0)),
            smem_tile(lambda i, f, te, nv: (n_tiles, 0, 0)),
            pl.BlockSpec(memory_space=pl.ANY),
            pl.BlockSpec((1, 1, d, tf), lambda i, f, te, nv: (layer, te[i], 0, wcol(i, f, te, nv))),
            pl.BlockSpec((1, 1, d, tf), lambda i, f, te, nv: (layer, te[i], 0, wcol(i, f, te, nv))),
            pl.BlockSpec((1, 1, tf, d), lambda i, f, te, nv: (layer, te[i], wcol(i, f, te, nv), 0)),
        ],
        out_specs=pl.BlockSpec(memory_space=pl.ANY),
        scratch_shapes=[
            pltpu.VMEM((2, tm, d), F32),
            pltpu.VMEM((tm, d), BF16),
            pltpu.VMEM((tm, d), F32),
            pltpu.VMEM((tm, d), F32),
            pltpu.SemaphoreType.DMA((2,)),
            pltpu.SemaphoreType.DMA(()),
        ],
    )
    return pl.pallas_call(
        functools.partial(_ffn_grouped_body, n_tiles, tm), grid_spec=grid_spec,
        out_shape=jax.ShapeDtypeStruct((yk_rows, d), F32),
        compiler_params=_cparams("arbitrary", "arbitrary"), name="ffn_grouped",
    )(tile_expert, n_live, src, src, dst, dst, h, w1, w3, w2)


def _combine_body(modulated, *refs):
    y_refs = refs[:TOP_K]
    refs = refs[TOP_K:]
    if modulated:
        wgt_ref, x_ref, g_ref, nw_ref, sc_ref, sh_ref, o_ref, hn_ref = refs
    else:
        wgt_ref, x_ref, g_ref, nw_ref, o_ref, hn_ref = refs
    w = wgt_ref[...]
    lane = lax.broadcasted_iota(I32, w.shape, 1)
    y = jnp.zeros(x_ref.shape, F32)
    for k in range(TOP_K):
        wk = jnp.sum(jnp.where(lane == k, w, 0.0), axis=-1, keepdims=True)
        y = y + wk * y_refs[k][...]
    x_new = x_ref[...] + g_ref[0] * y
    o_ref[...] = x_new
    if modulated:
        hn = _modulated_norm(x_new, nw_ref[...], sc_ref[0], sh_ref[0])
    else:
        hn = _modulated_norm(x_new, nw_ref[...], 0.0, 0.0)
    hn_ref[...] = hn.astype(hn_ref.dtype)


def _combine_residual(yk, wgt, x, gate, geom, first_row, next_norm):
    n_ctx_rows, seq, nb = geom
    tokens = wgt.shape[0]
    d = x.shape[1]
    tt = _tile(256, tokens, first_row if first_row else tokens, n_ctx_rows, seq)
    t0 = first_row // tt
    mrow = _mod_row_fn(tt, first_row, n_ctx_rows, seq, nb)
    mod_spec = pl.BlockSpec((1, 1, d), lambda i: (mrow(i), 0, 0))
    per_slot = tokens // tt
    slot_specs = [pl.BlockSpec((tt, d), functools.partial(lambda k, i: (i + k * per_slot, 0), k))
                  for k in range(TOP_K)]
    modulated = len(next_norm) == 3
    norm_specs = [pl.BlockSpec((1, d), lambda i: (0, 0))] + ([mod_spec, mod_spec] if modulated else [])
    norm_args = [next_norm[0].reshape(1, d)] + list(next_norm[1:])
    return pl.pallas_call(
        functools.partial(_combine_body, modulated), grid=(per_slot,),
        in_specs=slot_specs + [
            pl.BlockSpec((tt, LANES), lambda i: (i, 0)),
            pl.BlockSpec((tt, d), lambda i: (i + t0, 0)),
            mod_spec,
        ] + norm_specs,
        out_specs=[pl.BlockSpec((tt, d), lambda i: (i + t0, 0)), pl.BlockSpec((tt, d), lambda i: (i, 0))],
        out_shape=[jax.ShapeDtypeStruct(x.shape, F32),
                   jax.ShapeDtypeStruct((tokens, d), BF16 if modulated else F32)],
        input_output_aliases={TOP_K + 1: 0},
        compiler_params=_cparams("parallel"), name="moe_combine",
    )(*([yk] * TOP_K), wgt, x, gate, *norm_args)


def _moe(h, idx, wgt, w1, w3, w2, layer, x, gate, geom, first_row, next_norm):
    tokens, d = h.shape
    n_exp = w1.shape[1]
    tm = _tile(512, tokens)
    n_assign = TOP_K * tokens
    n_tiles = n_assign // tm + n_exp
    e_flat = idx[:, :TOP_K].reshape(n_assign)
    onehot = (e_flat[:, None] == jnp.arange(n_exp, dtype=I32)[None, :]).astype(I32)
    csum = jnp.cumsum(onehot, axis=0)
    rank = jnp.sum(csum * onehot, axis=1) - 1
    counts = csum[-1]
    padded = ((counts + tm - 1) // tm) * tm
    ends = jnp.cumsum(padded)
    starts = ends - padded
    pos = jnp.sum(onehot * starts[None, :], axis=1) + rank
    n_live = (ends[-1] // tm).astype(I32).reshape(1)
    tile_start = jnp.arange(n_tiles, dtype=I32) * tm
    tile_expert = jnp.sum((tile_start[:, None] >= ends[None, :]).astype(I32), axis=1)
    last_expert = jnp.max(jnp.where(counts > 0, jnp.arange(n_exp, dtype=I32), 0))
    tile_expert = jnp.minimum(tile_expert, last_expert).astype(I32)
    n_rows = n_tiles * tm
    assign_of_row = jnp.full((n_rows,), n_assign, I32).at[pos].set(jnp.arange(n_assign, dtype=I32))
    is_pad = assign_of_row >= n_assign
    src = jnp.where(is_pad, 0, assign_of_row // TOP_K)
    pad_ordinal = jnp.cumsum(is_pad.astype(I32)) - 1
    yk_row = jnp.where(is_pad, n_assign + pad_ordinal,
                       (assign_of_row % TOP_K) * tokens + assign_of_row // TOP_K)
    spare = n_rows + jnp.arange(tm, dtype=I32)
    dst = jnp.concatenate([spare, yk_row])

    yk = _ffn_grouped(h, src.reshape(n_tiles, 1, tm), dst.reshape(n_tiles + 1, 1, tm), tile_expert, n_live,
                      w1, w3, w2, layer, tm, n_rows + tm)
    return _combine_residual(yk, wgt, x, gate, geom, first_row, next_norm)


def _final_norm_body(x_ref, w_ref, o_ref):
    x = x_ref[...]
    ms = jnp.mean(x * x, axis=-1, keepdims=True)
    o_ref[...] = x * lax.rsqrt(ms + NORM_EPS) * w_ref[...]


def _final_norm(x, w, first_row):
    r, d = x.shape
    rows = r - first_row
    tm = _tile(256, rows, first_row)
    t0 = first_row // tm
    return pl.pallas_call(
        _final_norm_body, grid=(rows // tm,),
        in_specs=[pl.BlockSpec((tm, d), lambda i: (i + t0, 0)),
                  pl.BlockSpec((1, d), lambda i: (0, 0))],
        out_specs=pl.BlockSpec((tm, d), lambda i: (i, 0)),
        out_shape=jax.ShapeDtypeStruct((rows, d), F32),
        compiler_params=_cparams("parallel"), name="final_norm",
    )(x, w.reshape(1, d))


def kernel(x, c, ctx, c_ctx, w_mod, b_mod, norm1_w, norm2_w, w_in, conv_w, conv_b, conv_ln_w,
           conv_ln_b, ret_decay_logit, ret_gn_w, ret_gn_b, w_out, ffn_w1, ffn_w3, ffn_w2,
           moe_router, moe_router_b, moe_w1, moe_w3, moe_w2, final_w):
    nb, seq, d = x.shape
    n_ctx = ctx.shape[1]
    depth = w_mod.shape[0]
    cw = conv_w.shape[2]
    n_heads = ret_decay_logit.shape[2]
    dh = ret_gn_w.shape[1] // n_heads
    n_ctx_rows = nb * n_ctx
    geom = (n_ctx_rows, seq, nb)
    q_col0 = 2 * cw // LANES

    xs = jnp.concatenate([ctx.reshape(n_ctx_rows, d), x.reshape(nb * seq, d)], axis=0)

    rm = -(-(nb + 1) // SUBLANES) * SUBLANES
    c_all = jnp.zeros((rm, d), F32).at[:nb].set(c).at[nb].set(c_ctx)
    mods = _modulation(c_all, w_mod, b_mod).reshape(depth, rm, 6, 1, d)

    rope_tabs = _rope_tables(seq, dh)
    w_in_b = w_in.astype(BF16)
    w_out_b = w_out.astype(BF16)
    ffn_b = tuple(w.astype(BF16) for w in (ffn_w1, ffn_w3, ffn_w2))
    moe_b = tuple(w.astype(BF16) for w in (moe_w1, moe_w3, moe_w2))

    h = _norm_mod(xs, norm1_w[0], mods[0, :, 1], mods[0, :, 0], geom, 0, BF16)
    out = None
    for i in range(depth):
        last = i == depth - 1
        j = i // 2
        sh1, sc1, g1, sh2, sc2, g2 = (mods[i, :, m] for m in range(6))
        nxt = (norm1_w[i], sc1, sh1) if last else (norm1_w[i + 1], mods[i + 1, :, 1], mods[i + 1, :, 0])
        moe_layer = i % 2 == 1

        u = _matmul(h, w_in_b, i, BF16)

        ret_ctx, ctx_state = _retention(u, 0, n_ctx, nb, q_col0, ret_decay_logit[i],
                                        ret_gn_w[i], ret_gn_b[i], None, None)
        ret_lat, _ = _retention(u, n_ctx_rows, seq, nb, q_col0, ret_decay_logit[i],
                                ret_gn_w[i], ret_gn_b[i], rope_tabs, ctx_state)
        conv_lat = _conformer_conv(u, n_ctx_rows, seq, nb, conv_w[i], conv_b[i],
                                   conv_ln_w[i], conv_ln_b[i])
        first_row = n_ctx_rows if last else 0
        norm2 = (norm2_w[i], sc2, sh2)
        h2_dtype = F32 if moe_layer else BF16
        ctx_pair = None
        if not last:
            conv_ctx = _conformer_conv(u, 0, n_ctx, nb, conv_w[i], conv_b[i],
                                       conv_ln_w[i], conv_ln_b[i])
            ctx_pair = (conv_ctx, ret_ctx)
        xs, h2 = _out_proj_residual(ctx_pair, (conv_lat, ret_lat), w_out_b, i, xs, g1, geom,
                                    norm2, h2_dtype)

        if moe_layer:
            idx, wgt = _router(h2, moe_router[j], moe_router_b[j])
            xs, hn = _moe(h2, idx, wgt, *moe_b, j, xs, g2, geom, first_row, (final_w,) if last else nxt)
            if last:
                out = hn
        else:
            xs, hn = _ffn_dense(h2, *ffn_b, j, xs, g2, geom, first_row, nxt)
        h = hn

    if out is None:
        out = _final_norm(xs, final_w, n_ctx_rows)
    return out.reshape(nb, seq, d)
```
